```python
import jax, jax.numpy as jnp
from jax import lax
import numpy as np

D_MODEL = 1024
BATCH = 8
SEQ = 2048
DEPTH = 4

GLA_HEADS = 4
GLA_DK = D_MODEL // 2
GLA_DV = D_MODEL
GLA_HK = GLA_DK // GLA_HEADS
GLA_HV = GLA_DV // GLA_HEADS
GLA_RANK = 16
GLA_GATE_NORM = 16.0
GLA_CHUNK = 64
SWA_HEADS = 16
SWA_KV_HEADS = 2
HEAD_DIM = 64
SWA_GROUP = SWA_HEADS // SWA_KV_HEADS
SWA_DQ = SWA_HEADS * HEAD_DIM
SWA_DKV = SWA_KV_HEADS * HEAD_DIM
WINDOW = 128
ROPE_THETA = 10000.0
N_EXPERTS = 16
N_GROUPS = 4
EXPERTS_PER_GROUP = N_EXPERTS // N_GROUPS
TOP_K = 2
D_EXPERT = 512
N_MOD = 6
EPS = 1e-6

IN_SPLITS = (GLA_DK, GLA_DK, GLA_DV, GLA_DV, GLA_RANK, SWA_DQ, SWA_DKV, SWA_DKV, D_MODEL, D_MODEL)
D_IN = 2 * GLA_DK + 2 * GLA_DV + GLA_RANK + SWA_DQ + 2 * SWA_DKV + 2 * D_MODEL

kernel_name = "hybrid_gla_swa_sinks_grouped_moe_adaln"


def rmsnorm(x, g):
    x32 = x.astype(jnp.float32)
    y = x32 * lax.rsqrt(jnp.mean(x32 * x32, axis=-1, keepdims=True) + EPS)
    return (y * g.astype(jnp.float32)).astype(x.dtype)


def rope(x, pos):
    half = HEAD_DIM // 2
    inv_freq = jnp.power(ROPE_THETA, -jnp.arange(half, dtype=jnp.float32) / half)
    ang = pos.astype(jnp.float32)[..., None] * inv_freq
    cos = jnp.cos(ang)[:, :, None, :]
    sin = jnp.sin(ang)[:, :, None, :]
    x32 = x.astype(jnp.float32)
    x1, x2 = x32[..., :half], x32[..., half:]
    return jnp.concatenate([x1 * cos - x2 * sin, x2 * cos + x1 * sin], axis=-1).astype(x.dtype)


def gla_chunked(q, k, v, g):
    B, S, H, _ = q.shape
    nc = S // GLA_CHUNK

    def to_chunks(t):
        return t.astype(jnp.float32).reshape(B, nc, GLA_CHUNK, H, t.shape[-1]).transpose(1, 0, 3, 2, 4)

    q, k, v, g = to_chunks(q) * GLA_HK ** -0.5, to_chunks(k), to_chunks(v), to_chunks(g)
    b = jnp.cumsum(g, axis=-2)
    b_last = b[..., -1:, :]
    q_e = q * jnp.exp(b)
    k_intra = k * jnp.exp(-b)
    k_state = k * jnp.exp(b_last - b)
    causal = jnp.tril(jnp.ones((GLA_CHUNK, GLA_CHUNK), dtype=bool))
    att = jnp.where(causal, jnp.einsum('nbhtd,nbhsd->nbhts', q_e, k_intra), 0.0)
    o_intra = jnp.einsum('nbhts,nbhsv->nbhtv', att, v)
    decay = jnp.exp(b_last[..., 0, :])

    def step(state, inp):
        q_n, k_n, v_n, d_n = inp
        o_n = jnp.einsum('bhtd,bhdv->bhtv', q_n, state)
        state = d_n[..., None] * state + jnp.einsum('bhsd,bhsv->bhdv', k_n, v_n)
        return state, o_n

    s0 = jnp.zeros((B, H, GLA_HK, GLA_HV), jnp.float32)
    _, o_inter = lax.scan(step, s0, (q_e, k_state, v, decay))
    o = o_intra + o_inter
    return o.transpose(1, 0, 3, 2, 4).reshape(B, S, H, GLA_HV)


def swa_with_sinks(q, k, v, sinks):
    B, S = q.shape[0], q.shape[1]
    nb = S // WINDOW
    qb = q.astype(jnp.float32).reshape(B, nb, WINDOW, SWA_KV_HEADS, SWA_GROUP, HEAD_DIM) * HEAD_DIM ** -0.5
    kb = k.astype(jnp.float32).reshape(B, nb, WINDOW, SWA_KV_HEADS, HEAD_DIM)
    vb = v.astype(jnp.float32).reshape(B, nb, WINDOW, SWA_KV_HEADS, HEAD_DIM)

    def with_prev(t):
        prev = jnp.pad(t[:, :-1], ((0, 0), (1, 0), (0, 0), (0, 0), (0, 0)))
        return jnp.concatenate([prev, t], axis=2)

    kk, vv = with_prev(kb), with_prev(vb)
    s = jnp.einsum('bnqhgd,bnkhd->bhgnqk', qb, kk)
    qi = jnp.arange(WINDOW)[:, None]
    kj = jnp.arange(2 * WINDOW)[None, :]
    diff = WINDOW + qi - kj
    blk = jnp.arange(nb)[:, None, None]
    valid = (diff >= 0) & (diff < WINDOW) & (blk * WINDOW + kj - WINDOW >= 0)
    s = jnp.where(valid, s, -jnp.inf)
    sink = sinks.astype(jnp.float32).reshape(SWA_KV_HEADS, SWA_GROUP)[:, :, None, None, None]
    m = jnp.maximum(s.max(axis=-1, keepdims=True), sink)
    p = jnp.exp(s - m)
    denom = p.sum(axis=-1, keepdims=True) + jnp.exp(sink - m)
    o = jnp.einsum('bhgnqk,bnkhd->bnqhgd', p / denom, vv)
    return o.reshape(B, S, SWA_DQ)


def token_mixer(h, pos, w_in, a_w, a_b, gla_g, sinks, w_pa, w_pb, w_out):
    B, S, _ = h.shape
    split_points = [int(i) for i in np.cumsum(IN_SPLITS)[:-1]]
    q_a, k_a, v_a, r_a, lr_a, q_b, k_b, v_b, gt_a, gt_b = jnp.split(h @ w_in, split_points, axis=-1)

    def heads(t, d):
        return t.reshape(B, S, -1, d)

    g_a = jax.nn.log_sigmoid((lr_a @ a_w + a_b).astype(jnp.float32)) / GLA_GATE_NORM
    o_a = gla_chunked(heads(q_a, GLA_HK), heads(k_a, GLA_HK), heads(v_a, GLA_HV), heads(g_a, GLA_HK))
    o_a = rmsnorm(o_a, gla_g).astype(h.dtype).reshape(B, S, GLA_DV)
    y_a = o_a * jax.nn.silu(r_a)

    qr = rope(heads(q_b, HEAD_DIM), pos)
    kr = rope(heads(k_b, HEAD_DIM), pos)
    y_b = swa_with_sinks(qr, kr, heads(v_b, HEAD_DIM), sinks).astype(h.dtype)

    merged = jax.nn.sigmoid(gt_a) * (y_a @ w_pa) + jax.nn.sigmoid(gt_b) * (y_b @ w_pb)
    return merged @ w_out


def grouped_moe(h, router_w, router_b, w_gate, w_up, w_down):
    B, S, D = h.shape
    t = h.reshape(B * S, D)
    scores = jax.nn.sigmoid((t @ router_w).astype(jnp.float32))
    biased = scores + router_b.astype(jnp.float32)
    grp_scores = lax.top_k(biased.reshape(-1, N_GROUPS, EXPERTS_PER_GROUP), TOP_K)[0].sum(-1)
    grp = jnp.argmax(grp_scores, axis=-1)
    in_grp = (jnp.arange(N_EXPERTS) // EXPERTS_PER_GROUP)[None, :] == grp[:, None]
    _, idx = lax.top_k(jnp.where(in_grp, biased, -jnp.inf), TOP_K)
    w_sel = jnp.take_along_axis(scores, idx, axis=-1)
    w_sel = w_sel / w_sel.sum(axis=-1, keepdims=True)
    combine = jnp.sum(jax.nn.one_hot(idx, N_EXPERTS, dtype=jnp.float32) * w_sel[..., None], axis=1)
    hid = jax.nn.silu(jnp.einsum('nd,edf->nef', t, w_gate)) * jnp.einsum('nd,edf->nef', t, w_up)
    hid = hid * combine[..., None].astype(hid.dtype)
    out = jnp.einsum('nef,efd->nd', hid, w_down)
    return out.reshape(B, S, D)


def setup_inputs(seed: int = 0) -> dict:
    key = jax.random.key(seed)
    ks = jax.random.split(key, 21)
    L, D, E, F = DEPTH, D_MODEL, N_EXPERTS, D_EXPERT
    nrm = jax.random.normal
    return {
        "x": nrm(ks[0], (BATCH, SEQ, D), jnp.float32),
        "c": nrm(ks[1], (BATCH, D), jnp.float32),
        "positions": jnp.broadcast_to(jnp.arange(SEQ, dtype=jnp.int32), (BATCH, SEQ)),
        "ada_w": nrm(ks[2], (L, D, N_MOD * D), jnp.float32) * (0.3 * D ** -0.5),
        "ada_b": nrm(ks[3], (L, N_MOD * D), jnp.float32) * 0.01,
        "norm1_g": 1.0 + 0.05 * nrm(ks[4], (L, D), jnp.float32),
        "norm2_g": 1.0 + 0.05 * nrm(ks[5], (L, D), jnp.float32),
        "final_g": 1.0 + 0.05 * nrm(ks[6], (D,), jnp.float32),
        "w_in": nrm(ks[7], (L, D, D_IN), jnp.float32) * D ** -0.5,
        "gla_alpha_w": nrm(ks[8], (L, GLA_RANK, GLA_DK), jnp.float32) * GLA_RANK ** -0.5,
        "gla_alpha_b": 0.1 * nrm(ks[9], (L, GLA_DK), jnp.float32),
        "gla_norm_g": 1.0 + 0.05 * nrm(ks[10], (L, GLA_HV), jnp.float32),
        "swa_sinks": nrm(ks[11], (L, SWA_HEADS), jnp.float32),
        "w_pa": nrm(ks[12], (L, GLA_DV, D), jnp.float32) * GLA_DV ** -0.5,
        "w_pb": nrm(ks[13], (L, SWA_DQ, D), jnp.float32) * SWA_DQ ** -0.5,
        "w_out": nrm(ks[14], (L, D, D), jnp.float32) * D ** -0.5,
        "router_w": nrm(ks[15], (D, E), jnp.float32) * D ** -0.5,
        "router_b": 0.01 * nrm(ks[16], (E,), jnp.float32),
        "moe_w_gate": nrm(ks[17], (L, E, D, F), jnp.float32) * D ** -0.5,
        "moe_w_up": nrm(ks[18], (L, E, D, F), jnp.float32) * D ** -0.5,
        "moe_w_down": nrm(ks[19], (L, E, F, D), jnp.float32) * F ** -0.5,
    }


def reference(x, c, positions, ada_w, ada_b, norm1_g, norm2_g, final_g, w_in, gla_alpha_w, gla_alpha_b,
              gla_norm_g, swa_sinks, w_pa, w_pb, w_out, router_w, router_b, moe_w_gate, moe_w_up, moe_w_down):
    cond = jax.nn.silu(c)
    for l in range(DEPTH):
        mod = (cond @ ada_w[l] + ada_b[l])[:, None, :]
        sh1, sc1, g1, sh2, sc2, g2 = jnp.split(mod, N_MOD, axis=-1)
        h = rmsnorm(x, norm1_g[l]) * (1.0 + sc1) + sh1
        x = x + g1 * token_mixer(h, positions, w_in[l], gla_alpha_w[l], gla_alpha_b[l], gla_norm_g[l],
                                 swa_sinks[l], w_pa[l], w_pb[l], w_out[l])
        h = rmsnorm(x, norm2_g[l]) * (1.0 + sc2) + sh2
        x = x + g2 * grouped_moe(h, router_w, router_b, moe_w_gate[l], moe_w_up[l], moe_w_down[l])
    return rmsnorm(x, final_g)
```

```python
import functools

import jax
import jax.numpy as jnp
import numpy as np
from jax import lax
from jax.experimental import pallas as pl
from jax.experimental.pallas import tpu as pltpu

F32 = jnp.float32
BF16 = jnp.bfloat16

D_MODEL = 1024
DEPTH = 4
GLA_HEADS = 4
GLA_DK = D_MODEL // 2
GLA_DV = D_MODEL
GLA_HK = GLA_DK // GLA_HEADS
GLA_HV = GLA_DV // GLA_HEADS
GLA_RANK = 16
GLA_GATE_NORM = 16.0
GLA_CHUNK = 64
SWA_HEADS = 16
SWA_KV_HEADS = 2
HEAD_DIM = 64
SWA_GROUP = SWA_HEADS // SWA_KV_HEADS
SWA_DQ = SWA_HEADS * HEAD_DIM
SWA_DKV = SWA_KV_HEADS * HEAD_DIM
WINDOW = 128
ROPE_THETA = 10000.0
N_EXPERTS = 16
N_GROUPS = 4
EXPERTS_PER_GROUP = N_EXPERTS // N_GROUPS
D_EXPERT = 512
N_MOD = 6
EPS = 1e-6

LANES = 128
VMEM_LIMIT = 56 * 1024 * 1024
NEG_BIG = -1e30

_W_SEGS = (("qa", GLA_DK), ("ka", GLA_DK), ("va", GLA_DV), ("ra", GLA_DV), ("qb", SWA_DQ), ("kb", SWA_DKV),
           ("vb", SWA_DKV), ("gta", D_MODEL), ("gtb", D_MODEL), ("lr", LANES))
_W_OFF = {}
_o = 0
for _n, _w in _W_SEGS:
    _W_OFF[_n] = (_o, _o + _w)
    _o += _w
D_IN_PAD = _o

TM_PROJ = 512
TG_GLA = 256
TM_MOE = 1024
TS_ROPE = 2048


def _params(sem):
    return pltpu.CompilerParams(dimension_semantics=sem, vmem_limit_bytes=VMEM_LIMIT)


def _split_bf16(a):
    hi = a.astype(BF16)
    lo = (a - hi.astype(F32)).astype(BF16)
    return hi, lo


def _sigmoid(x):
    return 1.0 / (1.0 + jnp.exp(-x))


def _silu(x):
    return x * _sigmoid(x)


def _rope_table_kernel(pos_ref, invf_ref, sign_ref, cos_ref, sin_ref):
    ang = pos_ref[...].astype(F32) * invf_ref[...]
    cos_ref[...] = jnp.cos(ang)
    sin_ref[...] = jnp.sin(ang) * sign_ref[...]


def _rope_tables(positions):
    n = positions.size
    half = HEAD_DIM // 2
    inv_freq = jnp.power(ROPE_THETA, -jnp.arange(half, dtype=F32) / half)
    invf = jnp.tile(inv_freq, LANES // half)[None, :]
    sign = jnp.tile(jnp.concatenate([-jnp.ones((half,), F32), jnp.ones((half,), F32)]), LANES // HEAD_DIM)[None, :]
    pos = positions.reshape(n, 1)
    return pl.pallas_call(
        _rope_table_kernel,
        out_shape=(jax.ShapeDtypeStruct((n, LANES), F32), jax.ShapeDtypeStruct((n, LANES), F32)),
        grid=(n // TS_ROPE,),
        in_specs=[pl.BlockSpec((TS_ROPE, 1), lambda i: (i, 0)),
                  pl.BlockSpec((1, LANES), lambda i: (0, 0)),
                  pl.BlockSpec((1, LANES), lambda i: (0, 0))],
        out_specs=(pl.BlockSpec((TS_ROPE, LANES), lambda i: (i, 0)),
                   pl.BlockSpec((TS_ROPE, LANES), lambda i: (i, 0))),
        compiler_params=_params(("arbitrary",)),
        name="rope_tables",
    )(pos, invf, sign)


def _mod_kernel(c_ref, w_ref, b_ref, o_ref):
    cond = _silu(c_ref[...])
    chi, clo = _split_bf16(cond)
    whi, wlo = _split_bf16(w_ref[0])
    acc = jnp.dot(chi, whi, preferred_element_type=F32)
    acc += jnp.dot(chi, wlo, preferred_element_type=F32)
    acc += jnp.dot(clo, whi, preferred_element_type=F32)
    o_ref[0] = acc + b_ref[0]


def _modulation(c, ada_w, ada_b):
    nl, d, n6 = ada_w.shape
    b = c.shape[0]
    tn = 1536
    return pl.pallas_call(
        _mod_kernel,
        out_shape=jax.ShapeDtypeStruct((nl, b, n6), F32),
        grid=(nl, n6 // tn),
        in_specs=[pl.BlockSpec((b, d), lambda l, j: (0, 0)),
                  pl.BlockSpec((1, d, tn), lambda l, j: (l, 0, j)),
                  pl.BlockSpec((1, 1, tn), lambda l, j: (l, 0, j))],
        out_specs=pl.BlockSpec((1, b, tn), lambda l, j: (l, 0, j)),
        compiler_params=_params(("arbitrary", "arbitrary")),
        name="adaln_mod",
    )(c, ada_w, ada_b.reshape(nl, 1, n6))


def _norm_mod(x, g, sc, sh):
    ms = jnp.mean(x * x, axis=-1, keepdims=True)
    return (x * lax.rsqrt(ms + EPS) * g) * (1.0 + sc) + sh


def _rope(t, cos, sin_signed):
    w = t.shape[-1]
    up = pltpu.roll(t, w - HEAD_DIM // 2, 1)
    dn = pltpu.roll(t, HEAD_DIM // 2, 1)
    lane = lax.broadcasted_iota(jnp.int32, t.shape, 1)
    swapped = jnp.where(lane % HEAD_DIM < HEAD_DIM // 2, up, dn)
    reps = w // LANES
    if reps > 1:
        cos = jnp.concatenate([cos] * reps, axis=1)
        sin_signed = jnp.concatenate([sin_signed] * reps, axis=1)
    return t * cos + swapped * sin_signed


def _inproj_kernel(x_ref, g_ref, sc_ref, sh_ref, w_ref, cos_ref, sin_ref,
                   qa_ref, ka_ref, va_ref, ra_ref, qb_ref, kb_ref, vb_ref, gta_ref, gtb_ref, lr_ref):
    h = _norm_mod(x_ref[...], g_ref[...], sc_ref[0], sh_ref[0]).astype(BF16)

    def mm(name):
        c0, c1 = _W_OFF[name]
        return jnp.dot(h, w_ref[:, c0:c1], preferred_element_type=F32)

    qa_ref[...] = (mm("qa") * GLA_HK ** -0.5).astype(BF16)
    ka_ref[...] = mm("ka").astype(BF16)
    va_ref[...] = mm("va").astype(BF16)
    ra_ref[...] = mm("ra").astype(BF16)
    cos = cos_ref[...]
    sin = sin_ref[...]
    qb_ref[...] = (_rope(mm("qb"), cos, sin) * HEAD_DIM ** -0.5).astype(BF16)
    kb_ref[...] = _rope(mm("kb"), cos, sin).astype(BF16)
    vb_ref[...] = mm("vb").astype(BF16)
    gta_ref[...] = mm("gta").astype(BF16)
    gtb_ref[...] = mm("gtb").astype(BF16)
    lr_ref[...] = mm("lr")


def _inproj(x2, g, sc, sh, w, cos, sin, seq):
    n, d = x2.shape
    tm = TM_PROJ
    per_b = seq // tm
    row = lambda i: (i, 0)
    const = lambda i: (0, 0)
    bsel = lambda i: (i // per_b, 0, 0)
    widths = [(wd, BF16) for _, wd in _W_SEGS[:-1]] + [(LANES, F32)]
    return pl.pallas_call(
        _inproj_kernel,
        out_shape=tuple(jax.ShapeDtypeStruct((n, wd), dt) for wd, dt in widths),
        grid=(n // tm,),
        in_specs=[pl.BlockSpec((tm, d), row),
                  pl.BlockSpec((1, d), const),
                  pl.BlockSpec((1, 1, d), bsel),
                  pl.BlockSpec((1, 1, d), bsel),
                  pl.BlockSpec((d, D_IN_PAD), const, pipeline_mode=pl.Buffered(1)),
                  pl.BlockSpec((tm, LANES), row),
                  pl.BlockSpec((tm, LANES), row)],
        out_specs=tuple(pl.BlockSpec((tm, wd), row) for wd, _ in widths),
        compiler_params=_params(("arbitrary",)),
        name="norm_inproj",
    )(x2, g, sc, sh, w, cos, sin)


def _gla_kernel(q_ref, k_ref, v_ref, r_ref, lr_ref, aw_ref, ab_ref, gn_ref, tri_ref, o_ref, b_scr, st_scr):
    tg = q_ref.shape[1]
    c = GLA_CHUNK

    @pl.when(pl.program_id(1) == 0)
    def _():
        st_scr[...] = jnp.zeros_like(st_scr)

    pre = jnp.dot(lr_ref[0].astype(BF16), aw_ref[...], preferred_element_type=F32) + ab_ref[...]
    g = (jnp.minimum(pre, 0.0) - jnp.log1p(jnp.exp(-jnp.abs(pre)))) * (1.0 / GLA_GATE_NORM)
    ghi, glo = _split_bf16(g)
    tri = tri_ref[...]
    b_scr[...] = jnp.dot(tri, ghi, preferred_element_type=F32) + jnp.dot(tri, glo, preferred_element_type=F32)

    row = lax.broadcasted_iota(jnp.int32, (c, c), 0)
    col = lax.broadcasted_iota(jnp.int32, (c, c), 1)
    causal = col <= row
    gn = gn_ref[...]

    def chunk(ci, carry):
        r0 = pl.multiple_of(ci * c, c)
        rows = pl.ds(r0, c)
        for hh in range(GLA_HEADS):
            kc = slice(hh * GLA_HK, (hh + 1) * GLA_HK)
            vc = slice(hh * GLA_HV, (hh + 1) * GLA_HV)
            b = b_scr[rows, kc]
            b_last = b[c - 1:c, :]
            q = q_ref[0, rows, kc].astype(F32)
            k = k_ref[0, rows, kc].astype(F32)
            v = v_ref[0, rows, vc]
            q_e = (q * jnp.exp(b)).astype(BF16)
            k_i = (k * jnp.exp(-b)).astype(BF16)
            k_s = (k * jnp.exp(b_last - b)).astype(BF16)
            att = lax.dot_general(q_e, k_i, (((1,), (1,)), ((), ())), preferred_element_type=F32)
            att = jnp.where(causal, att, 0.0).astype(BF16)
            st = st_scr[hh]
            o = jnp.dot(att, v, preferred_element_type=F32)
            o += lax.dot_general(q_e, st.astype(BF16), (((1,), (1,)), ((), ())), preferred_element_type=F32)
            upd = lax.dot_general(v, k_s, (((0,), (0,)), ((), ())), preferred_element_type=F32)
            st_scr[hh] = st * jnp.exp(b_last) + upd
            ms = jnp.mean(o * o, axis=-1, keepdims=True)
            y = o * lax.rsqrt(ms + EPS) * gn
            r = r_ref[0, rows, vc].astype(F32)
            o_ref[0, rows, vc] = (y * _silu(r)).astype(BF16)
        return carry

    lax.fori_loop(0, tg // c, chunk, 0)


def _gla(qa, ka, va, ra, lr, aw, ab, gn, batch, seq):
    tg = TG_GLA
    blk = lambda b, s: (b, s, 0)
    const = lambda b, s: (0, 0)
    t = np.arange(tg)
    tri = jnp.asarray(((t[:, None] // GLA_CHUNK == t[None, :] // GLA_CHUNK) & (t[None, :] <= t[:, None])), BF16)
    r3 = lambda a: a.reshape(batch, seq, a.shape[-1])
    out = pl.pallas_call(
        _gla_kernel,
        out_shape=jax.ShapeDtypeStruct((batch, seq, GLA_DV), BF16),
        grid=(batch, seq // tg),
        in_specs=[pl.BlockSpec((1, tg, GLA_DK), blk),
                  pl.BlockSpec((1, tg, GLA_DK), blk),
                  pl.BlockSpec((1, tg, GLA_DV), blk),
                  pl.BlockSpec((1, tg, GLA_DV), blk),
                  pl.BlockSpec((1, tg, LANES), blk),
                  pl.BlockSpec((LANES, GLA_DK), const),
                  pl.BlockSpec((1, GLA_DK), const),
                  pl.BlockSpec((1, GLA_HV), const),
                  pl.BlockSpec((tg, tg), const)],
        out_specs=pl.BlockSpec((1, tg, GLA_DV), blk),
        scratch_shapes=[pltpu.VMEM((tg, GLA_DK), F32), pltpu.VMEM((GLA_HEADS, GLA_HV, GLA_HK), F32)],
        compiler_params=_params(("arbitrary", "arbitrary")),
        name="gla",
    )(r3(qa), r3(ka), r3(va), r3(ra), r3(lr), aw, ab, gn, tri)
    return out.reshape(batch * seq, GLA_DV)


def _swa_kernel(sink_ref, q_ref, kp_ref, kc_ref, vp_ref, vc_ref, o_ref):
    w = WINDOW
    blk = pl.program_id(1)
    rows = SWA_GROUP * w
    t = lax.broadcasted_iota(jnp.int32, (rows, 2 * w), 0) % w
    j = lax.broadcasted_iota(jnp.int32, (rows, 2 * w), 1)
    diff = w + t - j
    valid = (diff >= 0) & (diff < w) & (blk * w + j - w >= 0)
    for c in range(SWA_KV_HEADS):
        kv = slice(c * HEAD_DIM, (c + 1) * HEAD_DIM)
        kk = jnp.concatenate([kp_ref[0, :, kv], kc_ref[0, :, kv]], axis=0)
        vv = jnp.concatenate([vp_ref[0, :, kv], vc_ref[0, :, kv]], axis=0)
        heads = [c * SWA_GROUP + g for g in range(SWA_GROUP)]
        qs = jnp.concatenate([q_ref[0, :, h * HEAD_DIM:(h + 1) * HEAD_DIM] for h in heads], axis=0)
        sink = jnp.concatenate([jnp.full((w, 1), sink_ref[h], F32) for h in heads], axis=0)
        s = lax.dot_general(qs, kk, (((1,), (1,)), ((), ())), preferred_element_type=F32)
        s = jnp.where(valid, s, NEG_BIG)
        m = jnp.maximum(jnp.max(s, axis=-1, keepdims=True), sink)
        p = jnp.exp(s - m)
        denom = jnp.sum(p, axis=-1, keepdims=True) + jnp.exp(sink - m)
        o = jnp.dot((p / denom).astype(BF16), vv, preferred_element_type=F32)
        for g, h in enumerate(heads):
            o_ref[0, :, h * HEAD_DIM:(h + 1) * HEAD_DIM] = o[g * w:(g + 1) * w, :].astype(BF16)


def _swa(qb, kb, vb, sinks, batch, seq):
    w = WINDOW
    r3 = lambda a: a.reshape(batch, seq, a.shape[-1])
    cur = lambda b, s, sk: (b, s, 0)
    prev = lambda b, s, sk: (b, jnp.maximum(s - 1, 0), 0)
    out = pl.pallas_call(
        _swa_kernel,
        out_shape=jax.ShapeDtypeStruct((batch, seq, SWA_DQ), BF16),
        grid_spec=pltpu.PrefetchScalarGridSpec(
            num_scalar_prefetch=1,
            grid=(batch, seq // w),
            in_specs=[pl.BlockSpec((1, w, SWA_DQ), cur),
                      pl.BlockSpec((1, w, SWA_DKV), prev),
                      pl.BlockSpec((1, w, SWA_DKV), cur),
                      pl.BlockSpec((1, w, SWA_DKV), prev),
                      pl.BlockSpec((1, w, SWA_DKV), cur)],
            out_specs=pl.BlockSpec((1, w, SWA_DQ), cur)),
        compiler_params=_params(("arbitrary", "arbitrary")),
        name="swa",
    )(sinks, r3(qb), r3(kb), r3(kb), r3(vb), r3(vb))
    return out.reshape(batch * seq, SWA_DQ)


def _top2_sum(a):
    best = a[0] + a[1]
    for i in range(len(a)):
        for j in range(i + 1, len(a)):
            if (i, j) != (0, 1):
                best = jnp.maximum(best, a[i] + a[j])
    return best


def _route(logits_t, rb):
    scores = _sigmoid(logits_t)
    biased = scores + rb
    sc = [scores[e:e + 1, :] for e in range(N_EXPERTS)]
    bi = [biased[e:e + 1, :] for e in range(N_EXPERTS)]
    gs = [_top2_sum(bi[g * EXPERTS_PER_GROUP:(g + 1) * EXPERTS_PER_GROUP]) for g in range(N_GROUPS)]
    best = gs[0]
    for g in range(1, N_GROUPS):
        best = jnp.maximum(best, gs[g])
    taken = jnp.zeros_like(best, dtype=jnp.bool_)
    in_grp = []
    for g in range(N_GROUPS):
        sel = (gs[g] == best) & jnp.logical_not(taken)
        taken = taken | sel
        in_grp.append(sel)
    cand = [jnp.where(in_grp[e // EXPERTS_PER_GROUP], bi[e], -jnp.inf) for e in range(N_EXPERTS)]
    picked = [None] * N_EXPERTS
    chosen = [jnp.zeros_like(taken) for _ in range(N_EXPERTS)]
    for _ in range(2):
        top = cand[0]
        for e in range(1, N_EXPERTS):
            top = jnp.maximum(top, cand[e])
        done = jnp.zeros_like(taken)
        for e in range(N_EXPERTS):
            hit = (cand[e] == top) & jnp.logical_not(done)
            done = done | hit
            chosen[e] = chosen[e] | hit
            cand[e] = jnp.where(hit, -jnp.inf, cand[e])
    total = jnp.zeros_like(best)
    for e in range(N_EXPERTS):
        picked[e] = jnp.where(chosen[e], sc[e], 0.0)
        total = total + picked[e]
    return jnp.concatenate([p / total for p in picked], axis=0)


def _merge_kernel(x_ref, ya_ref, yb_ref, gta_ref, gtb_ref, wpa_ref, wpb_ref, wo_ref, g1_ref, n2_ref, sc2_ref, sh2_ref,
                  rwt_ref, rb_ref, xo_ref, h2_ref, comb_ref):
    pa = jnp.dot(ya_ref[...], wpa_ref[...], preferred_element_type=F32)
    pb = jnp.dot(yb_ref[...], wpb_ref[...], preferred_element_type=F32)
    merged = _sigmoid(gta_ref[...].astype(F32)) * pa + _sigmoid(gtb_ref[...].astype(F32)) * pb
    mix = jnp.dot(merged.astype(BF16), wo_ref[...], preferred_element_type=F32)
    x = x_ref[...] + g1_ref[0] * mix
    xo_ref[...] = x
    h2 = _norm_mod(x, n2_ref[...], sc2_ref[0], sh2_ref[0])
    h2_ref[...] = h2.astype(BF16)
    hhi, hlo = _split_bf16(h2)
    whi, wlo = _split_bf16(rwt_ref[...])
    nt = (((1,), (1,)), ((), ()))
    logits_t = (lax.dot_general(whi, hhi, nt, preferred_element_type=F32)
                + lax.dot_general(whi, hlo, nt, preferred_element_type=F32)
                + lax.dot_general(wlo, hhi, nt, preferred_element_type=F32))
    comb_t = _route(logits_t, rb_ref[...])
    pad = jnp.zeros((LANES - N_EXPERTS, comb_t.shape[1]), F32)
    comb_ref[...] = jnp.concatenate([comb_t, pad], axis=0).T


def _merge(x2, ya, yb, gta, gtb, wpa, wpb, wo, g1, n2, sc2, sh2, rwt, rb, seq):
    n, d = x2.shape
    tm = TM_PROJ
    per_b = seq // tm
    row = lambda i: (i, 0)
    const = lambda i: (0, 0)
    bsel = lambda i: (i // per_b, 0, 0)
    wspec = pl.BlockSpec((d, d), const, pipeline_mode=pl.Buffered(1))
    return pl.pallas_call(
        _merge_kernel,
        out_shape=(jax.ShapeDtypeStruct((n, d), F32), jax.ShapeDtypeStruct((n, d), BF16),
                   jax.ShapeDtypeStruct((n, LANES), F32)),
        grid=(n // tm,),
        in_specs=[pl.BlockSpec((tm, d), row)] * 5 + [wspec, wspec, wspec,
                  pl.BlockSpec((1, 1, d), bsel),
                  pl.BlockSpec((1, d), const),
                  pl.BlockSpec((1, 1, d), bsel),
                  pl.BlockSpec((1, 1, d), bsel),
                  pl.BlockSpec((N_EXPERTS, d), const),
                  pl.BlockSpec((N_EXPERTS, 1), const)],
        out_specs=(pl.BlockSpec((tm, d), row), pl.BlockSpec((tm, d), row), pl.BlockSpec((tm, LANES), row)),
        compiler_params=_params(("arbitrary",)),
        name="merge_route",
    )(x2, ya, yb, gta, gtb, wpa, wpb, wo, g1, n2, sc2, sh2, rwt, rb)


def _moe_kernel(h_ref, comb_ref, x_ref, g2_ref, wg_ref, wu_ref, wd_ref, fin_ref, o_ref, acc_ref, *, final_norm):
    e = pl.program_id(1)

    @pl.when(e == 0)
    def _():
        acc_ref[...] = jnp.zeros_like(acc_ref)

    h = h_ref[...]
    gate = jnp.dot(h, wg_ref[0], preferred_element_type=F32)
    up = jnp.dot(h, wu_ref[0], preferred_element_type=F32)
    lane = lax.broadcasted_iota(jnp.int32, comb_ref.shape, 1)
    w_e = jnp.sum(jnp.where(lane == e, comb_ref[...], 0.0), axis=-1, keepdims=True)
    hid = (_silu(gate) * up * w_e).astype(BF16)
    acc_ref[...] += jnp.dot(hid, wd_ref[0], preferred_element_type=F32)

    @pl.when(e == pl.num_programs(1) - 1)
    def _():
        x = x_ref[...] + g2_ref[0] * acc_ref[...]
        if final_norm:
            ms = jnp.mean(x * x, axis=-1, keepdims=True)
            x = x * lax.rsqrt(ms + EPS) * fin_ref[...]
        o_ref[...] = x


def _moe(h2, comb, x2, g2, wg, wu, wd, fin, seq, final_norm):
    n, d = x2.shape
    tm = TM_MOE
    per_b = seq // tm
    row = lambda i, e: (i, 0)
    return pl.pallas_call(
        functools.partial(_moe_kernel, final_norm=final_norm),
        out_shape=jax.ShapeDtypeStruct((n, d), F32),
        grid=(n // tm, N_EXPERTS),
        in_specs=[pl.BlockSpec((tm, d), row),
                  pl.BlockSpec((tm, LANES), row),
                  pl.BlockSpec((tm, d), row),
                  pl.BlockSpec((1, 1, d), lambda i, e: (i // per_b, 0, 0)),
                  pl.BlockSpec((1, d, D_EXPERT), lambda i, e: (e, 0, 0)),
                  pl.BlockSpec((1, d, D_EXPERT), lambda i, e: (e, 0, 0)),
                  pl.BlockSpec((1, D_EXPERT, d), lambda i, e: (e, 0, 0)),
                  pl.BlockSpec((1, d), lambda i, e: (0, 0))],
        out_specs=pl.BlockSpec((tm, d), row),
        scratch_shapes=[pltpu.VMEM((tm, d), F32)],
        compiler_params=_params(("arbitrary", "arbitrary")),
        name="moe",
    )(h2, comb, x2, g2, wg, wu, wd, fin)


def _pack_w_in(w_in_l):
    parts = jnp.split(w_in_l, [int(i) for i in np.cumsum(
        (GLA_DK, GLA_DK, GLA_DV, GLA_DV, GLA_RANK, SWA_DQ, SWA_DKV, SWA_DKV, D_MODEL))], axis=-1)
    q_a, k_a, v_a, r_a, lr_a, q_b, k_b, v_b, gt_a, gt_b = parts
    lr_pad = jnp.pad(lr_a, ((0, 0), (0, LANES - GLA_RANK)))
    return jnp.concatenate([q_a, k_a, v_a, r_a, q_b, k_b, v_b, gt_a, gt_b, lr_pad], axis=-1).astype(BF16)


def kernel(x, c, positions, ada_w, ada_b, norm1_g, norm2_g, final_g, w_in, gla_alpha_w, gla_alpha_b, gla_norm_g,
           swa_sinks, w_pa, w_pb, w_out, router_w, router_b, moe_w_gate, moe_w_up, moe_w_down):
    batch, seq, d = x.shape
    n = batch * seq
    cos, sin = _rope_tables(positions)
    mod = _modulation(c, ada_w, ada_b)
    rwt = router_w.T
    rb = router_b.reshape(N_EXPERTS, 1)
    x2 = x.reshape(n, d)
    for l in range(DEPTH):
        m = mod[l].reshape(batch, N_MOD, 1, d)
        sh1, sc1, g1, sh2, sc2, g2 = (m[:, i] for i in range(N_MOD))
        qa, ka, va, ra, qb, kb, vb, gta, gtb, lr = _inproj(
            x2, norm1_g[l][None, :], sc1, sh1, _pack_w_in(w_in[l]), cos, sin, seq)
        aw = jnp.pad(gla_alpha_w[l], ((0, LANES - GLA_RANK), (0, 0))).astype(BF16)
        ya = _gla(qa, ka, va, ra, lr, aw, gla_alpha_b[l][None, :], gla_norm_g[l][None, :], batch, seq)
        yb = _swa(qb, kb, vb, swa_sinks[l], batch, seq)
        x2, h2, comb = _merge(x2, ya, yb, gta, gtb, w_pa[l].astype(BF16), w_pb[l].astype(BF16),
                              w_out[l].astype(BF16), g1, norm2_g[l][None, :], sc2, sh2, rwt, rb, seq)
        x2 = _moe(h2, comb, x2, g2, moe_w_gate[l].astype(BF16), moe_w_up[l].astype(BF16),
                  moe_w_down[l].astype(BF16), final_g[None, :], seq, l == DEPTH - 1)
    return x2.reshape(batch, seq, d)
```

```python
import functools

import jax
import jax.numpy as jnp
import numpy as np
from jax import lax
from jax.experimental import pallas as pl
from jax.experimental.pallas import tpu as pltpu

F32 = jnp.float32
BF16 = jnp.bfloat16

D_MODEL = 1024
DEPTH = 4
GLA_HEADS = 4
GLA_DK = D_MODEL // 2
GLA_DV = D_MODEL
GLA_HK = GLA_DK // GLA_HEADS
GLA_HV = GLA_DV // GLA_HEADS
GLA_RANK = 16
GLA_GATE_NORM = 16.0
GLA_CHUNK = 64
SWA_HEADS = 16
SWA_KV_HEADS = 2
HEAD_DIM = 64
SWA_GROUP = SWA_HEADS // SWA_KV_HEADS
SWA_DQ = SWA_HEADS * HEAD_DIM
SWA_DKV = SWA_KV_HEADS * HEAD_DIM
WINDOW = 128
ROPE_THETA = 10000.0
N_EXPERTS = 16
N_GROUPS = 4
EXPERTS_PER_GROUP = N_EXPERTS // N_GROUPS
D_EXPERT = 512
N_MOD = 6
EPS = 1e-6

LANES = 128
VMEM_LIMIT = 56 * 1024 * 1024
NEG_BIG = -1e30

_W_SEGS = (("qa", GLA_DK), ("ka", GLA_DK), ("va", GLA_DV), ("ra", GLA_DV), ("qb", SWA_DQ), ("kb", SWA_DKV),
           ("vb", SWA_DKV), ("gta", D_MODEL), ("gtb", D_MODEL), ("lr", LANES))
_W_OFF = {}
_o = 0
for _n, _w in _W_SEGS:
    _W_OFF[_n] = (_o, _o + _w)
    _o += _w
D_IN_PAD = _o

TM_PROJ = 512
TG_GLA = 256
TM_MOE = 256
TS_ROPE = 2048


def _params(sem):
    return pltpu.CompilerParams(dimension_semantics=sem, vmem_limit_bytes=VMEM_LIMIT)


def _split_bf16(a):
    hi = a.astype(BF16)
    lo = (a - hi.astype(F32)).astype(BF16)
    return hi, lo


def _sigmoid(x):
    return 1.0 / (1.0 + jnp.exp(-x))


def _silu(x):
    return x * _sigmoid(x)


def _rope_table_kernel(pos_ref, invf_ref, sign_ref, cos_ref, sin_ref):
    ang = pos_ref[...].astype(F32) * invf_ref[...]
    cos_ref[...] = jnp.cos(ang)
    sin_ref[...] = jnp.sin(ang) * sign_ref[...]


def _rope_tables(positions):
    n = positions.size
    half = HEAD_DIM // 2
    inv_freq = jnp.power(ROPE_THETA, -jnp.arange(half, dtype=F32) / half)
    invf = jnp.tile(inv_freq, LANES // half)[None, :]
    sign = jnp.tile(jnp.concatenate([-jnp.ones((half,), F32), jnp.ones((half,), F32)]), LANES // HEAD_DIM)[None, :]
    pos = positions.reshape(n, 1)
    return pl.pallas_call(
        _rope_table_kernel,
        out_shape=(jax.ShapeDtypeStruct((n, LANES), F32), jax.ShapeDtypeStruct((n, LANES), F32)),
        grid=(n // TS_ROPE,),
        in_specs=[pl.BlockSpec((TS_ROPE, 1), lambda i: (i, 0)),
                  pl.BlockSpec((1, LANES), lambda i: (0, 0)),
                  pl.BlockSpec((1, LANES), lambda i: (0, 0))],
        out_specs=(pl.BlockSpec((TS_ROPE, LANES), lambda i: (i, 0)),
                   pl.BlockSpec((TS_ROPE, LANES), lambda i: (i, 0))),
        compiler_params=_params(("arbitrary",)),
        name="rope_tables",
    )(pos, invf, sign)


def _mod_kernel(c_ref, w_ref, b_ref, o_ref):
    cond = _silu(c_ref[...])
    chi, clo = _split_bf16(cond)
    whi, wlo = _split_bf16(w_ref[0])
    acc = jnp.dot(chi, whi, preferred_element_type=F32)
    acc += jnp.dot(chi, wlo, preferred_element_type=F32)
    acc += jnp.dot(clo, whi, preferred_element_type=F32)
    o_ref[0] = acc + b_ref[0]


def _modulation(c, ada_w, ada_b):
    nl, d, n6 = ada_w.shape
    b = c.shape[0]
    tn = 1536
    return pl.pallas_call(
        _mod_kernel,
        out_shape=jax.ShapeDtypeStruct((nl, b, n6), F32),
        grid=(nl, n6 // tn),
        in_specs=[pl.BlockSpec((b, d), lambda l, j: (0, 0)),
                  pl.BlockSpec((1, d, tn), lambda l, j: (l, 0, j)),
                  pl.BlockSpec((1, 1, tn), lambda l, j: (l, 0, j))],
        out_specs=pl.BlockSpec((1, b, tn), lambda l, j: (l, 0, j)),
        compiler_params=_params(("arbitrary", "arbitrary")),
        name="adaln_mod",
    )(c, ada_w, ada_b.reshape(nl, 1, n6))


def _norm_mod(x, g, sc, sh):
    ms = jnp.mean(x * x, axis=-1, keepdims=True)
    return (x * lax.rsqrt(ms + EPS) * g) * (1.0 + sc) + sh


def _rope(t, cos, sin_signed):
    w = t.shape[-1]
    up = pltpu.roll(t, w - HEAD_DIM // 2, 1)
    dn = pltpu.roll(t, HEAD_DIM // 2, 1)
    lane = lax.broadcasted_iota(jnp.int32, t.shape, 1)
    swapped = jnp.where(lane % HEAD_DIM < HEAD_DIM // 2, up, dn)
    reps = w // LANES
    if reps > 1:
        cos = jnp.concatenate([cos] * reps, axis=1)
        sin_signed = jnp.concatenate([sin_signed] * reps, axis=1)
    return t * cos + swapped * sin_signed


def _inproj_kernel(*refs, has_y):
    if has_y:
        x_ref, y_ref, g2_ref = refs[:3]
        refs = refs[3:]
        x = x_ref[...] + g2_ref[0] * y_ref[...]
    else:
        x_ref = refs[0]
        refs = refs[1:]
        x = x_ref[...]
    g_ref, sc_ref, sh_ref, w_ref, cos_ref, sin_ref = refs[:6]
    outs = refs[6:]
    if has_y:
        outs[0][...] = x
        outs = outs[1:]
    qa_ref, ka_ref, va_ref, ra_ref, qb_ref, kb_ref, vb_ref, gta_ref, gtb_ref, lr_ref = outs
    h = _norm_mod(x, g_ref[...], sc_ref[0], sh_ref[0]).astype(BF16)

    def mm(name):
        c0, c1 = _W_OFF[name]
        return jnp.dot(h, w_ref[:, c0:c1], preferred_element_type=F32)

    qa_ref[...] = (mm("qa") * GLA_HK ** -0.5).astype(BF16)
    ka_ref[...] = mm("ka").astype(BF16)
    va_ref[...] = mm("va").astype(BF16)
    ra_ref[...] = mm("ra").astype(BF16)
    cos = cos_ref[...]
    sin = sin_ref[...]
    qb_ref[...] = (_rope(mm("qb"), cos, sin) * HEAD_DIM ** -0.5).astype(BF16)
    kb_ref[...] = _rope(mm("kb"), cos, sin).astype(BF16)
    vb_ref[...] = mm("vb").astype(BF16)
    gta_ref[...] = mm("gta").astype(BF16)
    gtb_ref[...] = mm("gtb").astype(BF16)
    lr_ref[...] = mm("lr")


def _inproj(x2, y, g2, g, sc, sh, w, cos, sin, seq):
    n, d = x2.shape
    tm = TM_PROJ
    per_b = seq // tm
    row = lambda i: (i, 0)
    const = lambda i: (0, 0)
    bsel = lambda i: (i // per_b, 0, 0)
    has_y = y is not None
    widths = [(wd, BF16) for _, wd in _W_SEGS[:-1]] + [(LANES, F32)]
    if has_y:
        widths = [(d, F32)] + widths
    resid_specs = [pl.BlockSpec((tm, d), row), pl.BlockSpec((1, 1, d), bsel)] if has_y else []
    resid_args = (y, g2) if has_y else ()
    return pl.pallas_call(
        functools.partial(_inproj_kernel, has_y=has_y),
        out_shape=tuple(jax.ShapeDtypeStruct((n, wd), dt) for wd, dt in widths),
        grid=(n // tm,),
        in_specs=[pl.BlockSpec((tm, d), row)] + resid_specs + [
                  pl.BlockSpec((1, d), const),
                  pl.BlockSpec((1, 1, d), bsel),
                  pl.BlockSpec((1, 1, d), bsel),
                  pl.BlockSpec((d, D_IN_PAD), const, pipeline_mode=pl.Buffered(1)),
                  pl.BlockSpec((tm, LANES), row),
                  pl.BlockSpec((tm, LANES), row)],
        out_specs=tuple(pl.BlockSpec((tm, wd), row) for wd, _ in widths),
        compiler_params=_params(("arbitrary",)),
        name="norm_inproj",
    )(x2, *resid_args, g, sc, sh, w, cos, sin)


def _gla_kernel(q_ref, k_ref, v_ref, r_ref, lr_ref, aw_ref, ab_ref, gn_ref, tri_ref, o_ref, b_scr, st_scr):
    tg = q_ref.shape[1]
    c = GLA_CHUNK

    @pl.when(pl.program_id(1) == 0)
    def _():
        st_scr[...] = jnp.zeros_like(st_scr)

    pre = jnp.dot(lr_ref[0].astype(BF16), aw_ref[...], preferred_element_type=F32) + ab_ref[...]
    g = (jnp.minimum(pre, 0.0) - jnp.log1p(jnp.exp(-jnp.abs(pre)))) * (1.0 / GLA_GATE_NORM)
    ghi, glo = _split_bf16(g)
    tri = tri_ref[...]
    b_scr[...] = jnp.dot(tri, ghi, preferred_element_type=F32) + jnp.dot(tri, glo, preferred_element_type=F32)

    row = lax.broadcasted_iota(jnp.int32, (c, c), 0)
    col = lax.broadcasted_iota(jnp.int32, (c, c), 1)
    causal = col <= row
    gn = gn_ref[...]

    def chunk(ci, carry):
        r0 = pl.multiple_of(ci * c, c)
        rows = pl.ds(r0, c)
        for hh in range(GLA_HEADS):
            kc = slice(hh * GLA_HK, (hh + 1) * GLA_HK)
            vc = slice(hh * GLA_HV, (hh + 1) * GLA_HV)
            b = b_scr[rows, kc]
            b_last = b[c - 1:c, :]
            q = q_ref[0, rows, kc].astype(F32)
            k = k_ref[0, rows, kc].astype(F32)
            v = v_ref[0, rows, vc]
            q_e = (q * jnp.exp(b)).astype(BF16)
            k_i = (k * jnp.exp(-b)).astype(BF16)
            k_s = (k * jnp.exp(b_last - b)).astype(BF16)
            att = lax.dot_general(q_e, k_i, (((1,), (1,)), ((), ())), preferred_element_type=F32)
            att = jnp.where(causal, att, 0.0).astype(BF16)
            st = st_scr[hh]
            o = jnp.dot(att, v, preferred_element_type=F32)
            o += lax.dot_general(q_e, st.astype(BF16), (((1,), (1,)), ((), ())), preferred_element_type=F32)
            upd = lax.dot_general(v, k_s, (((0,), (0,)), ((), ())), preferred_element_type=F32)
            st_scr[hh] = st * jnp.exp(b_last) + upd
            ms = jnp.mean(o * o, axis=-1, keepdims=True)
            y = o * lax.rsqrt(ms + EPS) * gn
            r = r_ref[0, rows, vc].astype(F32)
            o_ref[0, rows, vc] = (y * _silu(r)).astype(BF16)
        return carry

    lax.fori_loop(0, tg // c, chunk, 0)


def _gla(qa, ka, va, ra, lr, aw, ab, gn, batch, seq):
    tg = TG_GLA
    blk = lambda b, s: (b, s, 0)
    const = lambda b, s: (0, 0)
    t = np.arange(tg)
    tri = jnp.asarray(((t[:, None] // GLA_CHUNK == t[None, :] // GLA_CHUNK) & (t[None, :] <= t[:, None])), BF16)
    r3 = lambda a: a.reshape(batch, seq, a.shape[-1])
    out = pl.pallas_call(
        _gla_kernel,
        out_shape=jax.ShapeDtypeStruct((batch, seq, GLA_DV), BF16),
        grid=(batch, seq // tg),
        in_specs=[pl.BlockSpec((1, tg, GLA_DK), blk),
                  pl.BlockSpec((1, tg, GLA_DK), blk),
                  pl.BlockSpec((1, tg, GLA_DV), blk),
                  pl.BlockSpec((1, tg, GLA_DV), blk),
                  pl.BlockSpec((1, tg, LANES), blk),
                  pl.BlockSpec((LANES, GLA_DK), const),
                  pl.BlockSpec((1, GLA_DK), const),
                  pl.BlockSpec((1, GLA_HV), const),
                  pl.BlockSpec((tg, tg), const)],
        out_specs=pl.BlockSpec((1, tg, GLA_DV), blk),
        scratch_shapes=[pltpu.VMEM((tg, GLA_DK), F32), pltpu.VMEM((GLA_HEADS, GLA_HV, GLA_HK), F32)],
        compiler_params=_params(("arbitrary", "arbitrary")),
        name="gla",
    )(r3(qa), r3(ka), r3(va), r3(ra), r3(lr), aw, ab, gn, tri)
    return out.reshape(batch * seq, GLA_DV)


def _swa_kernel(sink_ref, q_ref, kp_ref, kc_ref, vp_ref, vc_ref, o_ref, s_scr, p_scr):
    w = WINDOW
    blk = pl.program_id(1)
    t = lax.broadcasted_iota(jnp.int32, (w, 2 * w), 0)
    j = lax.broadcasted_iota(jnp.int32, (w, 2 * w), 1)
    diff = w + t - j
    valid = (diff >= 0) & (diff < w) & (blk * w + j - w >= 0)
    bias = jnp.where(valid, 0.0, NEG_BIG)

    lane = lax.broadcasted_iota(jnp.int32, (2 * w, LANES), 1)
    low = lane < HEAD_DIM

    def halves(prev_ref, cur_ref):
        both = jnp.concatenate([prev_ref[0], cur_ref[0]], axis=0)
        swapped = pltpu.roll(both, HEAD_DIM, 1)
        zero = jnp.zeros_like(both)
        kv0 = (jnp.where(low, both, zero), jnp.where(low, zero, swapped))
        kv1 = (jnp.where(low, swapped, zero), jnp.where(low, zero, both))
        return kv0, kv1

    k_halves = halves(kp_ref, kc_ref)
    v_halves = halves(vp_ref, vc_ref)
    pairs = SWA_GROUP // 2
    nt = (((1,), (1,)), ((), ()))
    for c in range(SWA_KV_HEADS):
        q4 = jnp.concatenate([q_ref[0, :, (c * pairs + p) * LANES:(c * pairs + p + 1) * LANES]
                              for p in range(pairs)], axis=0)
        for par in range(2):
            s_scr[par] = lax.dot_general(q4, k_halves[c][par], nt, preferred_element_type=F32)
        for par in range(2):
            for p in range(pairs):
                sink = sink_ref[c * SWA_GROUP + 2 * p + par]
                rows = slice(p * w, (p + 1) * w)
                s = s_scr[par, rows, :] + bias
                m = jnp.maximum(jnp.max(s, axis=-1, keepdims=True), sink)
                e = jnp.exp(s - m)
                denom = jnp.sum(e, axis=-1, keepdims=True) + jnp.exp(sink - m)
                p_scr[par, rows, :] = (e * (1.0 / denom)).astype(BF16)
        o4 = (jnp.dot(p_scr[0], v_halves[c][0], preferred_element_type=F32)
              + jnp.dot(p_scr[1], v_halves[c][1], preferred_element_type=F32))
        for p in range(pairs):
            o_ref[0, :, (c * pairs + p) * LANES:(c * pairs + p + 1) * LANES] = o4[p * w:(p + 1) * w].astype(BF16)


def _swa(qb, kb, vb, sinks, batch, seq):
    w = WINDOW
    r3 = lambda a: a.reshape(batch, seq, a.shape[-1])
    cur = lambda b, s, sk: (b, s, 0)
    prev = lambda b, s, sk: (b, jnp.maximum(s - 1, 0), 0)
    out = pl.pallas_call(
        _swa_kernel,
        out_shape=jax.ShapeDtypeStruct((batch, seq, SWA_DQ), BF16),
        grid_spec=pltpu.PrefetchScalarGridSpec(
            num_scalar_prefetch=1,
            grid=(batch, seq // w),
            in_specs=[pl.BlockSpec((1, w, SWA_DQ), cur),
                      pl.BlockSpec((1, w, SWA_DKV), prev),
                      pl.BlockSpec((1, w, SWA_DKV), cur),
                      pl.BlockSpec((1, w, SWA_DKV), prev),
                      pl.BlockSpec((1, w, SWA_DKV), cur)],
            out_specs=pl.BlockSpec((1, w, SWA_DQ), cur),
            scratch_shapes=[pltpu.VMEM((2, SWA_GROUP // 2 * w, 2 * w), F32),
                            pltpu.VMEM((2, SWA_GROUP // 2 * w, 2 * w), BF16)]),
        compiler_params=_params(("arbitrary", "arbitrary")),
        name="swa",
    )(sinks, r3(qb), r3(kb), r3(kb), r3(vb), r3(vb))
    return out.reshape(batch * seq, SWA_DQ)


def _top2_sum(a):
    best = a[0] + a[1]
    for i in range(len(a)):
        for j in range(i + 1, len(a)):
            if (i, j) != (0, 1):
                best = jnp.maximum(best, a[i] + a[j])
    return best


_PAIR_SLOTS = ((0, 1), (2, 1), (2, 0), (3, 0), (3, 1), (3, 2))
N_CLASSES = N_GROUPS * len(_PAIR_SLOTS)
CLASS_ROWS = 32
_CLASS_EA = tuple(g * EXPERTS_PER_GROUP + a for g in range(N_GROUPS) for a, _ in _PAIR_SLOTS)
_CLASS_EB = tuple(g * EXPERTS_PER_GROUP + b for g in range(N_GROUPS) for _, b in _PAIR_SLOTS)


def _route(logits_t, rb):
    scores = _sigmoid(logits_t)
    biased = scores + rb
    sc = [scores[e:e + 1, :] for e in range(N_EXPERTS)]
    bi = [biased[e:e + 1, :] for e in range(N_EXPERTS)]
    gs = [_top2_sum(bi[g * EXPERTS_PER_GROUP:(g + 1) * EXPERTS_PER_GROUP]) for g in range(N_GROUPS)]
    best = gs[0]
    for g in range(1, N_GROUPS):
        best = jnp.maximum(best, gs[g])
    taken = jnp.zeros_like(best, dtype=jnp.bool_)
    in_grp = []
    for g in range(N_GROUPS):
        sel = (gs[g] == best) & jnp.logical_not(taken)
        taken = taken | sel
        in_grp.append(sel)
    cand = [jnp.where(in_grp[e // EXPERTS_PER_GROUP], bi[e], -jnp.inf) for e in range(N_EXPERTS)]
    picked = [None] * N_EXPERTS
    chosen = [jnp.zeros_like(taken) for _ in range(N_EXPERTS)]
    for _ in range(2):
        top = cand[0]
        for e in range(1, N_EXPERTS):
            top = jnp.maximum(top, cand[e])
        done = jnp.zeros_like(taken)
        for e in range(N_EXPERTS):
            hit = (cand[e] == top) & jnp.logical_not(done)
            done = done | hit
            chosen[e] = chosen[e] | hit
            cand[e] = jnp.where(hit, -jnp.inf, cand[e])
    total = jnp.zeros_like(best)
    for e in range(N_EXPERTS):
        picked[e] = jnp.where(chosen[e], sc[e], 0.0)
        total = total + picked[e]
    ind = [chosen[_CLASS_EA[k]] & chosen[_CLASS_EB[k]] for k in range(N_CLASSES)]
    w_a = jnp.zeros_like(best)
    w_b = jnp.zeros_like(best)
    for k in range(N_CLASSES):
        w_a = w_a + jnp.where(ind[k], picked[_CLASS_EA[k]], 0.0)
        w_b = w_b + jnp.where(ind[k], picked[_CLASS_EB[k]], 0.0)
    ind_rows = [jnp.where(i, 1.0, 0.0) for i in ind] + [jnp.zeros_like(best)] * (CLASS_ROWS - N_CLASSES)
    return jnp.concatenate(ind_rows, axis=0), w_a / total, w_b / total


def _merge_kernel(x_ref, ya_ref, yb_ref, gta_ref, gtb_ref, wpa_ref, wpb_ref, wo_ref, g1_ref, n2_ref, sc2_ref, sh2_ref,
                  rwt_ref, rb_ref, tri_ref, xo_ref, hx_ref, info_ref, cnt_ref, carry_scr):
    @pl.when(pl.program_id(0) == 0)
    def _():
        carry_scr[...] = jnp.zeros_like(carry_scr)

    pa = jnp.dot(ya_ref[...], wpa_ref[...], preferred_element_type=F32)
    pb = jnp.dot(yb_ref[...], wpb_ref[...], preferred_element_type=F32)
    merged = _sigmoid(gta_ref[...].astype(F32)) * pa + _sigmoid(gtb_ref[...].astype(F32)) * pb
    mix = jnp.dot(merged.astype(BF16), wo_ref[...], preferred_element_type=F32)
    x = x_ref[...] + g1_ref[0] * mix
    xo_ref[...] = x
    h2 = _norm_mod(x, n2_ref[...], sc2_ref[0], sh2_ref[0])
    d = h2.shape[1]
    tm = h2.shape[0]
    hhi, hlo = _split_bf16(h2)
    whi, wlo = _split_bf16(rwt_ref[...])
    nt = (((1,), (1,)), ((), ()))
    logits_t = (lax.dot_general(whi, hhi, nt, preferred_element_type=F32)
                + lax.dot_general(whi, hlo, nt, preferred_element_type=F32)
                + lax.dot_general(wlo, hhi, nt, preferred_element_type=F32))
    ind, w_a, w_b = _route(logits_t, rb_ref[...])
    hx_ref[:, :d] = h2
    hx_ref[:, d:] = jnp.concatenate([w_a, w_b, jnp.zeros((LANES - 2, tm), F32)], axis=0).T
    carry = carry_scr[:, 0:1]
    cum = jnp.dot(ind.astype(BF16), tri_ref[...], preferred_element_type=F32)
    k_col = lax.broadcasted_iota(jnp.int32, (CLASS_ROWS, 1), 0).astype(F32)
    cls = jnp.sum(ind * k_col, axis=0, keepdims=True)
    rank = jnp.sum(ind * (cum + carry), axis=0, keepdims=True) - 1.0
    info = jnp.concatenate([cls, rank, jnp.zeros((6, tm), F32)], axis=0)
    info_ref[0] = info.astype(jnp.int32)
    carry = carry + jnp.sum(ind, axis=1, keepdims=True)
    carry_scr[...] = jnp.broadcast_to(carry, carry_scr.shape)
    cnt_ref[...] = jnp.broadcast_to(carry, cnt_ref.shape).astype(jnp.int32)


def _merge(x2, ya, yb, gta, gtb, wpa, wpb, wo, g1, n2, sc2, sh2, rwt, rb, seq):
    n, d = x2.shape
    tm = TM_PROJ
    per_b = seq // tm
    row = lambda i: (i, 0)
    const = lambda i: (0, 0)
    bsel = lambda i: (i // per_b, 0, 0)
    wspec = pl.BlockSpec((d, d), const, pipeline_mode=pl.Buffered(1))
    t = np.arange(tm)
    tri = jnp.asarray(t[:, None] <= t[None, :], BF16)
    return pl.pallas_call(
        _merge_kernel,
        out_shape=(jax.ShapeDtypeStruct((n, d), F32), jax.ShapeDtypeStruct((n, d + LANES), F32),
                   jax.ShapeDtypeStruct((n // tm, 8, tm), jnp.int32),
                   jax.ShapeDtypeStruct((CLASS_ROWS, LANES), jnp.int32)),
        grid=(n // tm,),
        in_specs=[pl.BlockSpec((tm, d), row)] * 5 + [wspec, wspec, wspec,
                  pl.BlockSpec((1, 1, d), bsel),
                  pl.BlockSpec((1, d), const),
                  pl.BlockSpec((1, 1, d), bsel),
                  pl.BlockSpec((1, 1, d), bsel),
                  pl.BlockSpec((N_EXPERTS, d), const),
                  pl.BlockSpec((N_EXPERTS, 1), const),
                  pl.BlockSpec((tm, tm), const)],
        out_specs=(pl.BlockSpec((tm, d), row), pl.BlockSpec((tm, d + LANES), row),
                   pl.BlockSpec((1, 8, tm), lambda i: (i, 0, 0)),
                   pl.BlockSpec((CLASS_ROWS, LANES), const)),
        scratch_shapes=[pltpu.VMEM((CLASS_ROWS, LANES), F32)],
        compiler_params=_params(("arbitrary",)),
        name="merge_route",
    )(x2, ya, yb, gta, gtb, wpa, wpb, wo, g1, n2, sc2, sh2, rwt, rb, tri)


def _moe_plan(counts, n):
    tm = TM_MOE
    tiles = n // tm + N_CLASSES
    padded = (counts + tm - 1) // tm * tm
    ends = jnp.cumsum(padded)
    bounds = jnp.concatenate([jnp.zeros((1,), jnp.int32), ends]).astype(jnp.int32)
    nvt = ends[-1] // tm
    pos = jnp.minimum(jnp.arange(tiles, dtype=jnp.int32), nvt - 1) * tm
    k = jnp.searchsorted(ends, pos, side="right").astype(jnp.int32)
    ea = jnp.asarray(_CLASS_EA + (0,) * (CLASS_ROWS - N_CLASSES), jnp.int32)[k]
    eb = jnp.asarray(_CLASS_EB + (0,) * (CLASS_ROWS - N_CLASSES), jnp.int32)[k]
    first = jnp.ones((1,), jnp.int32)
    cha = jnp.concatenate([first, (ea[1:] != ea[:-1]).astype(jnp.int32)])
    chb = jnp.concatenate([first, (eb[1:] != eb[:-1]).astype(jnp.int32)])
    return ea, eb, cha, chb, bounds, nvt.reshape(1).astype(jnp.int32)


def _moe_kernel(ea_ref, eb_ref, cha_ref, chb_ref, bnd_ref, cnt_ref, nvt_ref, cls_ref, rank_ref,
                hx_hbm, wga_ref, wua_ref, wda_ref, wgb_ref, wub_ref, wdb_ref, y_hbm,
                buf, obuf, wg_bf, wu_bf, wd_bf, perm, gsem, ssem, *, n_tokens):
    i = pl.program_id(0)
    tm = TM_MOE
    d = obuf.shape[1]
    nvt = nvt_ref[0]

    def gather_start(tile, slot):
        for r in range(tm):
            src = jnp.minimum(perm[tile * tm + r], n_tokens - 1)
            pltpu.make_async_copy(hx_hbm.at[pl.ds(src, 1), :], buf.at[slot, pl.ds(r, 1), :], gsem.at[slot]).start()

    def gather_wait(slot):
        pltpu.make_async_copy(hx_hbm.at[pl.ds(0, tm), :], buf.at[slot], gsem.at[slot]).wait()

    def scatter_start(tile):
        for r in range(tm):
            dst = perm[tile * tm + r]
            pltpu.make_async_copy(obuf.at[pl.ds(r, 1), :], y_hbm.at[pl.ds(dst, 1), :], ssem.at[0]).start()

    def scatter_wait():
        pltpu.make_async_copy(obuf, y_hbm.at[pl.ds(0, tm), :], ssem.at[0]).wait()

    @pl.when(i == 0)
    def _():
        obuf[...] = jnp.zeros_like(obuf)
        tail = pltpu.make_async_copy(obuf, y_hbm.at[pl.ds(n_tokens, tm), :], ssem.at[0])
        tail.start()
        tail.wait()

        def place(t, carry):
            perm[bnd_ref[cls_ref[t]] + rank_ref[t]] = t
            return carry

        lax.fori_loop(0, n_tokens, place, 0, unroll=8)

        def pad(p, carry):
            perm[p] = n_tokens + lax.rem(p, tm)
            return carry

        for k in range(N_CLASSES):
            lax.fori_loop(bnd_ref[k] + cnt_ref[k], bnd_ref[k + 1], pad, 0)
        lax.fori_loop(bnd_ref[N_CLASSES], bnd_ref[N_CLASSES] + tm, pad, 0)
        gather_start(0, 0)

    @pl.when((i < nvt) & (cha_ref[i] == 1))
    def _():
        wg_bf[0] = wga_ref[0, 0].astype(BF16)
        wu_bf[0] = wua_ref[0, 0].astype(BF16)
        wd_bf[0] = wda_ref[0, 0].astype(BF16)

    @pl.when((i < nvt) & (chb_ref[i] == 1))
    def _():
        wg_bf[1] = wgb_ref[0, 0].astype(BF16)
        wu_bf[1] = wub_ref[0, 0].astype(BF16)
        wd_bf[1] = wdb_ref[0, 0].astype(BF16)

    @pl.when(i < nvt)
    def _():
        slot = lax.rem(i, 2)
        gather_wait(slot)
        gather_start(i + 1, 1 - slot)
        hx = buf[slot]
        h = hx[:, :d].astype(BF16)
        acc = None
        for s in range(2):
            w = hx[:, d + s:d + s + 1]
            gate = jnp.dot(h, wg_bf[s], preferred_element_type=F32)
            up = jnp.dot(h, wu_bf[s], preferred_element_type=F32)
            hid = (_silu(gate) * up * w).astype(BF16)
            part = jnp.dot(hid, wd_bf[s], preferred_element_type=F32)
            acc = part if acc is None else acc + part

        @pl.when(i > 0)
        def _():
            scatter_wait()

        obuf[...] = acc
        scatter_start(i)

    @pl.when(i == nvt)
    def _():
        gather_wait(lax.rem(nvt, 2))
        scatter_wait()


def _moe(hx, info, cnt, w_gate, w_up, w_down, layer):
    n, dx = hx.shape
    d = dx - LANES
    tm = TM_MOE
    tiles = n // tm + N_CLASSES
    counts = cnt[:, 0]
    ea, eb, cha, chb, bounds, nvt = _moe_plan(counts, n)
    cls = info[:, 0, :].reshape(n)
    rank = info[:, 1, :].reshape(n)
    f = w_gate.shape[-1]
    wa = lambda i, ea, eb, *_: (layer, ea[i], 0, 0)
    wb = lambda i, ea, eb, *_: (layer, eb[i], 0, 0)
    return pl.pallas_call(
        functools.partial(_moe_kernel, n_tokens=n),
        out_shape=jax.ShapeDtypeStruct((n + tm, d), F32),
        grid_spec=pltpu.PrefetchScalarGridSpec(
            num_scalar_prefetch=9,
            grid=(tiles,),
            in_specs=[pl.BlockSpec(memory_space=pl.ANY),
                      pl.BlockSpec((1, 1, d, f), wa), pl.BlockSpec((1, 1, d, f), wa), pl.BlockSpec((1, 1, f, d), wa),
                      pl.BlockSpec((1, 1, d, f), wb), pl.BlockSpec((1, 1, d, f), wb), pl.BlockSpec((1, 1, f, d), wb)],
            out_specs=pl.BlockSpec(memory_space=pl.ANY),
            scratch_shapes=[pltpu.VMEM((2, tm, dx), F32), pltpu.VMEM((tm, d), F32),
                            pltpu.VMEM((2, d, f), BF16), pltpu.VMEM((2, d, f), BF16), pltpu.VMEM((2, f, d), BF16),
                            pltpu.SMEM((tiles * tm,), jnp.int32),
                            pltpu.SemaphoreType.DMA((2,)), pltpu.SemaphoreType.DMA((1,))]),
        compiler_params=_params(("arbitrary",)),
        name="moe",
    )(ea, eb, cha, chb, bounds, counts, nvt, cls, rank, hx, w_gate, w_up, w_down, w_gate, w_up, w_down)


def _final_kernel(x_ref, y_ref, g2_ref, fin_ref, o_ref):
    x = x_ref[...] + g2_ref[0] * y_ref[...]
    ms = jnp.mean(x * x, axis=-1, keepdims=True)
    o_ref[...] = x * lax.rsqrt(ms + EPS) * fin_ref[...]


def _final(x2, y, g2, fin, seq):
    n, d = x2.shape
    tm = TM_PROJ
    per_b = seq // tm
    row = lambda i: (i, 0)
    return pl.pallas_call(
        _final_kernel,
        out_shape=jax.ShapeDtypeStruct((n, d), F32),
        grid=(n // tm,),
        in_specs=[pl.BlockSpec((tm, d), row), pl.BlockSpec((tm, d), row),
                  pl.BlockSpec((1, 1, d), lambda i: (i // per_b, 0, 0)),
                  pl.BlockSpec((1, d), lambda i: (0, 0))],
        out_specs=pl.BlockSpec((tm, d), row),
        compiler_params=_params(("arbitrary",)),
        name="final_norm",
    )(x2, y, g2, fin)


def _pack_w_in(w_in_l):
    parts = jnp.split(w_in_l, [int(i) for i in np.cumsum(
        (GLA_DK, GLA_DK, GLA_DV, GLA_DV, GLA_RANK, SWA_DQ, SWA_DKV, SWA_DKV, D_MODEL))], axis=-1)
    q_a, k_a, v_a, r_a, lr_a, q_b, k_b, v_b, gt_a, gt_b = parts
    lr_pad = jnp.pad(lr_a, ((0, 0), (0, LANES - GLA_RANK)))
    return jnp.concatenate([q_a, k_a, v_a, r_a, q_b, k_b, v_b, gt_a, gt_b, lr_pad], axis=-1).astype(BF16)


def kernel(x, c, positions, ada_w, ada_b, norm1_g, norm2_g, final_g, w_in, gla_alpha_w, gla_alpha_b, gla_norm_g,
           swa_sinks, w_pa, w_pb, w_out, router_w, router_b, moe_w_gate, moe_w_up, moe_w_down):
    batch, seq, d = x.shape
    n = batch * seq
    cos, sin = _rope_tables(positions)
    mod = _modulation(c, ada_w, ada_b)
    rwt = router_w.T
    rb = router_b.reshape(N_EXPERTS, 1)
    x2 = x.reshape(n, d)
    y = None
    g2 = None
    for l in range(DEPTH):
        m = mod[l].reshape(batch, N_MOD, 1, d)
        sh1, sc1, g1, sh2, sc2, g2_l = (m[:, i] for i in range(N_MOD))
        outs = _inproj(x2, y, g2, norm1_g[l][None, :], sc1, sh1, _pack_w_in(w_in[l]), cos, sin, seq)
        if y is not None:
            x2, outs = outs[0], outs[1:]
        qa, ka, va, ra, qb, kb, vb, gta, gtb, lr = outs
        aw = jnp.pad(gla_alpha_w[l], ((0, LANES - GLA_RANK), (0, 0))).astype(BF16)
        ya = _gla(qa, ka, va, ra, lr, aw, gla_alpha_b[l][None, :], gla_norm_g[l][None, :], batch, seq)
        yb = _swa(qb, kb, vb, swa_sinks[l], batch, seq)
        x2, hx, info, cnt = _merge(x2, ya, yb, gta, gtb, w_pa[l].astype(BF16), w_pb[l].astype(BF16),
                                   w_out[l].astype(BF16), g1, norm2_g[l][None, :], sc2, sh2, rwt, rb, seq)
        y = _moe(hx, info, cnt, moe_w_gate, moe_w_up, moe_w_down, l)
        g2 = g2_l
    return _final(x2, y, g2, final_g[None, :], seq).reshape(batch, seq, d)
```

```python
import functools

import jax
import jax.numpy as jnp
import numpy as np
from jax import lax
from jax.experimental import pallas as pl
from jax.experimental.pallas import tpu as pltpu

F32 = jnp.float32
BF16 = jnp.bfloat16

D_MODEL = 1024
DEPTH = 4
GLA_HEADS = 4
GLA_DK = D_MODEL // 2
GLA_DV = D_MODEL
GLA_HK = GLA_DK // GLA_HEADS
GLA_HV = GLA_DV // GLA_HEADS
GLA_RANK = 16
GLA_GATE_NORM = 16.0
GLA_CHUNK = 64
SWA_HEADS = 16
SWA_KV_HEADS = 2
HEAD_DIM = 64
SWA_GROUP = SWA_HEADS // SWA_KV_HEADS
SWA_DQ = SWA_HEADS * HEAD_DIM
SWA_DKV = SWA_KV_HEADS * HEAD_DIM
WINDOW = 128
ROPE_THETA = 10000.0
N_EXPERTS = 16
N_GROUPS = 4
EXPERTS_PER_GROUP = N_EXPERTS // N_GROUPS
D_EXPERT = 512
N_MOD = 6
EPS = 1e-6

LANES = 128
VMEM_LIMIT = 56 * 1024 * 1024
NEG_BIG = -1e30

_W_SEGS = (("qa", GLA_DK), ("ka", GLA_DK), ("va", GLA_DV), ("ra", GLA_DV), ("qb", SWA_DQ), ("kb", SWA_DKV),
           ("vb", SWA_DKV), ("gta", D_MODEL), ("gtb", D_MODEL), ("lr", LANES))
_W_OFF = {}
_o = 0
for _n, _w in _W_SEGS:
    _W_OFF[_n] = (_o, _o + _w)
    _o += _w
D_IN_PAD = _o

TM_PROJ = 512
TG_GLA = 256
TM_MOE = 256
TS_ROPE = 2048


def _params(sem):
    return pltpu.CompilerParams(dimension_semantics=sem, vmem_limit_bytes=VMEM_LIMIT)


def _split_bf16(a):
    hi = a.astype(BF16)
    lo = (a - hi.astype(F32)).astype(BF16)
    return hi, lo


def _sigmoid(x):
    return 1.0 / (1.0 + jnp.exp(-x))


def _silu(x):
    return x * _sigmoid(x)


def _rope_table_kernel(pos_ref, invf_ref, sign_ref, cos_ref, sin_ref):
    ang = pos_ref[...].astype(F32) * invf_ref[...]
    cos_ref[...] = jnp.cos(ang)
    sin_ref[...] = jnp.sin(ang) * sign_ref[...]


def _rope_tables(positions):
    n = positions.size
    half = HEAD_DIM // 2
    inv_freq = jnp.power(ROPE_THETA, -jnp.arange(half, dtype=F32) / half)
    invf = jnp.tile(inv_freq, LANES // half)[None, :]
    sign = jnp.tile(jnp.concatenate([-jnp.ones((half,), F32), jnp.ones((half,), F32)]), LANES // HEAD_DIM)[None, :]
    pos = positions.reshape(n, 1)
    return pl.pallas_call(
        _rope_table_kernel,
        out_shape=(jax.ShapeDtypeStruct((n, LANES), F32), jax.ShapeDtypeStruct((n, LANES), F32)),
        grid=(n // TS_ROPE,),
        in_specs=[pl.BlockSpec((TS_ROPE, 1), lambda i: (i, 0)),
                  pl.BlockSpec((1, LANES), lambda i: (0, 0)),
                  pl.BlockSpec((1, LANES), lambda i: (0, 0))],
        out_specs=(pl.BlockSpec((TS_ROPE, LANES), lambda i: (i, 0)),
                   pl.BlockSpec((TS_ROPE, LANES), lambda i: (i, 0))),
        compiler_params=_params(("arbitrary",)),
        name="rope_tables",
    )(pos, invf, sign)


def _mod_kernel(c_ref, w_ref, b_ref, o_ref):
    cond = _silu(c_ref[...])
    chi, clo = _split_bf16(cond)
    whi, wlo = _split_bf16(w_ref[0])
    acc = jnp.dot(chi, whi, preferred_element_type=F32)
    acc += jnp.dot(chi, wlo, preferred_element_type=F32)
    acc += jnp.dot(clo, whi, preferred_element_type=F32)
    o_ref[0] = acc + b_ref[0]


def _modulation(c, ada_w, ada_b):
    nl, d, n6 = ada_w.shape
    b = c.shape[0]
    tn = 1536
    return pl.pallas_call(
        _mod_kernel,
        out_shape=jax.ShapeDtypeStruct((nl, b, n6), F32),
        grid=(nl, n6 // tn),
        in_specs=[pl.BlockSpec((b, d), lambda l, j: (0, 0)),
                  pl.BlockSpec((1, d, tn), lambda l, j: (l, 0, j)),
                  pl.BlockSpec((1, 1, tn), lambda l, j: (l, 0, j))],
        out_specs=pl.BlockSpec((1, b, tn), lambda l, j: (l, 0, j)),
        compiler_params=_params(("arbitrary", "arbitrary")),
        name="adaln_mod",
    )(c, ada_w, ada_b.reshape(nl, 1, n6))


def _norm_mod(x, g, sc, sh):
    ms = jnp.mean(x * x, axis=-1, keepdims=True)
    return (x * lax.rsqrt(ms + EPS) * g) * (1.0 + sc) + sh


def _rope(t, cos, sin_signed):
    w = t.shape[-1]
    up = pltpu.roll(t, w - HEAD_DIM // 2, 1)
    dn = pltpu.roll(t, HEAD_DIM // 2, 1)
    lane = lax.broadcasted_iota(jnp.int32, t.shape, 1)
    swapped = jnp.where(lane % HEAD_DIM < HEAD_DIM // 2, up, dn)
    reps = w // LANES
    if reps > 1:
        cos = jnp.concatenate([cos] * reps, axis=1)
        sin_signed = jnp.concatenate([sin_signed] * reps, axis=1)
    return t * cos + swapped * sin_signed


def _gathered_rows(pos_ref, y_hbm, ybuf, ysem):
    i = pl.program_id(0)
    steps = pl.num_programs(0)
    tm = ybuf.shape[1]

    def start(step, slot):
        for r in range(tm):
            pltpu.make_async_copy(y_hbm.at[pl.ds(pos_ref[step * tm + r], 1), :], ybuf.at[slot, pl.ds(r, 1), :],
                                  ysem.at[slot]).start()

    def wait(slot):
        pltpu.make_async_copy(y_hbm.at[pl.ds(0, tm), :], ybuf.at[slot], ysem.at[slot]).wait()

    @pl.when(i == 0)
    def _():
        start(0, 0)

    slot = lax.rem(i, 2)
    wait(slot)
    start(lax.rem(i + 1, steps), 1 - slot)

    def finish():
        @pl.when(i == steps - 1)
        def _():
            wait(1 - slot)

    return ybuf[slot], finish


def _inproj_kernel(*refs, has_y):
    finish = None
    if has_y:
        pos_ref, x_ref, y_hbm, g2_ref = refs[:4]
        ybuf, ysem = refs[-2:]
        refs = refs[4:-2]
        rows, finish = _gathered_rows(pos_ref, y_hbm, ybuf, ysem)
        x = x_ref[...] + g2_ref[0] * rows
    else:
        x_ref = refs[0]
        refs = refs[1:]
        x = x_ref[...]
    g_ref, sc_ref, sh_ref, w_ref, cos_ref, sin_ref = refs[:6]
    outs = refs[6:]
    if has_y:
        outs[0][...] = x
        outs = outs[1:]
    qa_ref, ka_ref, va_ref, ra_ref, qb_ref, kb_ref, vb_ref, gta_ref, gtb_ref, lr_ref = outs
    h = _norm_mod(x, g_ref[...], sc_ref[0], sh_ref[0]).astype(BF16)

    def mm(name):
        c0, c1 = _W_OFF[name]
        return jnp.dot(h, w_ref[:, c0:c1], preferred_element_type=F32)

    qa_ref[...] = (mm("qa") * GLA_HK ** -0.5).astype(BF16)
    ka_ref[...] = mm("ka").astype(BF16)
    va_ref[...] = mm("va").astype(BF16)
    ra_ref[...] = mm("ra").astype(BF16)
    cos = cos_ref[...]
    sin = sin_ref[...]
    qb_ref[...] = (_rope(mm("qb"), cos, sin) * HEAD_DIM ** -0.5).astype(BF16)
    kb_ref[...] = _rope(mm("kb"), cos, sin).astype(BF16)
    vb_ref[...] = mm("vb").astype(BF16)
    gta_ref[...] = mm("gta").astype(BF16)
    gtb_ref[...] = mm("gtb").astype(BF16)
    lr_ref[...] = mm("lr")
    if finish is not None:
        finish()


def _inproj(x2, y, pos, g2, g, sc, sh, w, cos, sin, seq):
    n, d = x2.shape
    tm = TM_PROJ
    per_b = seq // tm
    row = lambda i, *_: (i, 0)
    const = lambda i, *_: (0, 0)
    bsel = lambda i, *_: (i // per_b, 0, 0)
    has_y = y is not None
    widths = [(wd, BF16) for _, wd in _W_SEGS[:-1]] + [(LANES, F32)]
    if has_y:
        widths = [(d, F32)] + widths
    resid_specs = [pl.BlockSpec(memory_space=pl.ANY), pl.BlockSpec((1, 1, d), bsel)] if has_y else []
    args = (pos, x2, y, g2) if has_y else (x2,)
    scratch = [pltpu.VMEM((2, tm, d), F32), pltpu.SemaphoreType.DMA((2,))] if has_y else []
    return pl.pallas_call(
        functools.partial(_inproj_kernel, has_y=has_y),
        out_shape=tuple(jax.ShapeDtypeStruct((n, wd), dt) for wd, dt in widths),
        grid_spec=pltpu.PrefetchScalarGridSpec(
            num_scalar_prefetch=1 if has_y else 0,
            grid=(n // tm,),
            in_specs=[pl.BlockSpec((tm, d), row)] + resid_specs + [
                      pl.BlockSpec((1, d), const),
                      pl.BlockSpec((1, 1, d), bsel),
                      pl.BlockSpec((1, 1, d), bsel),
                      pl.BlockSpec((d, D_IN_PAD), const, pipeline_mode=pl.Buffered(1)),
                      pl.BlockSpec((tm, LANES), row),
                      pl.BlockSpec((tm, LANES), row)],
            out_specs=tuple(pl.BlockSpec((tm, wd), row) for wd, _ in widths),
            scratch_shapes=scratch),
        compiler_params=_params(("arbitrary",)),
        name="norm_inproj",
    )(*args, g, sc, sh, w, cos, sin)


def _gla_kernel(q_ref, k_ref, v_ref, r_ref, lr_ref, aw_ref, ab_ref, gn_ref, tri_ref, o_ref, b_scr, st_scr):
    tg = q_ref.shape[1]
    c = GLA_CHUNK

    @pl.when(pl.program_id(1) == 0)
    def _():
        st_scr[...] = jnp.zeros_like(st_scr)

    pre = jnp.dot(lr_ref[0].astype(BF16), aw_ref[...], preferred_element_type=F32) + ab_ref[...]
    g = (jnp.minimum(pre, 0.0) - jnp.log1p(jnp.exp(-jnp.abs(pre)))) * (1.0 / GLA_GATE_NORM)
    ghi, glo = _split_bf16(g)
    tri = tri_ref[...]
    b_scr[...] = jnp.dot(tri, ghi, preferred_element_type=F32) + jnp.dot(tri, glo, preferred_element_type=F32)

    row = lax.broadcasted_iota(jnp.int32, (c, c), 0)
    col = lax.broadcasted_iota(jnp.int32, (c, c), 1)
    causal = col <= row
    gn = gn_ref[...]

    def chunk(ci, carry):
        r0 = pl.multiple_of(ci * c, c)
        rows = pl.ds(r0, c)
        for hh in range(GLA_HEADS):
            kc = slice(hh * GLA_HK, (hh + 1) * GLA_HK)
            vc = slice(hh * GLA_HV, (hh + 1) * GLA_HV)
            b = b_scr[rows, kc]
            b_last = b[c - 1:c, :]
            q = q_ref[0, rows, kc].astype(F32)
            k = k_ref[0, rows, kc].astype(F32)
            v = v_ref[0, rows, vc]
            q_e = (q * jnp.exp(b)).astype(BF16)
            k_i = (k * jnp.exp(-b)).astype(BF16)
            k_s = (k * jnp.exp(b_last - b)).astype(BF16)
            att = lax.dot_general(q_e, k_i, (((1,), (1,)), ((), ())), preferred_element_type=F32)
            att = jnp.where(causal, att, 0.0).astype(BF16)
            st = st_scr[hh]
            o = jnp.dot(att, v, preferred_element_type=F32)
            o += lax.dot_general(q_e, st.astype(BF16), (((1,), (1,)), ((), ())), preferred_element_type=F32)
            upd = lax.dot_general(v, k_s, (((0,), (0,)), ((), ())), preferred_element_type=F32)
            st_scr[hh] = st * jnp.exp(b_last) + upd
            ms = jnp.mean(o * o, axis=-1, keepdims=True)
            y = o * lax.rsqrt(ms + EPS) * gn
            r = r_ref[0, rows, vc].astype(F32)
            o_ref[0, rows, vc] = (y * _silu(r)).astype(BF16)
        return carry

    lax.fori_loop(0, tg // c, chunk, 0, unroll=True)


def _gla(qa, ka, va, ra, lr, aw, ab, gn, batch, seq):
    tg = TG_GLA
    blk = lambda b, s: (b, s, 0)
    const = lambda b, s: (0, 0)
    t = np.arange(tg)
    tri = jnp.asarray(((t[:, None] // GLA_CHUNK == t[None, :] // GLA_CHUNK) & (t[None, :] <= t[:, None])), BF16)
    r3 = lambda a: a.reshape(batch, seq, a.shape[-1])
    out = pl.pallas_call(
        _gla_kernel,
        out_shape=jax.ShapeDtypeStruct((batch, seq, GLA_DV), BF16),
        grid=(batch, seq // tg),
        in_specs=[pl.BlockSpec((1, tg, GLA_DK), blk),
                  pl.BlockSpec((1, tg, GLA_DK), blk),
                  pl.BlockSpec((1, tg, GLA_DV), blk),
                  pl.BlockSpec((1, tg, GLA_DV), blk),
                  pl.BlockSpec((1, tg, LANES), blk),
                  pl.BlockSpec((LANES, GLA_DK), const),
                  pl.BlockSpec((1, GLA_DK), const),
                  pl.BlockSpec((1, GLA_HV), const),
                  pl.BlockSpec((tg, tg), const)],
        out_specs=pl.BlockSpec((1, tg, GLA_DV), blk),
        scratch_shapes=[pltpu.VMEM((tg, GLA_DK), F32), pltpu.VMEM((GLA_HEADS, GLA_HV, GLA_HK), F32)],
        compiler_params=_params(("arbitrary", "arbitrary")),
        name="gla",
    )(r3(qa), r3(ka), r3(va), r3(ra), r3(lr), aw, ab, gn, tri)
    return out.reshape(batch * seq, GLA_DV)


def _swa_kernel(sink_ref, q_ref, kp_ref, kc_ref, vp_ref, vc_ref, o_ref, s_scr, p_scr):
    w = WINDOW
    blk = pl.program_id(1)
    t = lax.broadcasted_iota(jnp.int32, (w, 2 * w), 0)
    j = lax.broadcasted_iota(jnp.int32, (w, 2 * w), 1)
    diff = w + t - j
    valid = (diff >= 0) & (diff < w) & (blk * w + j - w >= 0)
    bias = jnp.where(valid, 0.0, NEG_BIG)

    lane = lax.broadcasted_iota(jnp.int32, (2 * w, LANES), 1)
    low = lane < HEAD_DIM

    def halves(prev_ref, cur_ref):
        both = jnp.concatenate([prev_ref[0], cur_ref[0]], axis=0)
        swapped = pltpu.roll(both, HEAD_DIM, 1)
        zero = jnp.zeros_like(both)
        kv0 = (jnp.where(low, both, zero), jnp.where(low, zero, swapped))
        kv1 = (jnp.where(low, swapped, zero), jnp.where(low, zero, both))
        return kv0, kv1

    k_halves = halves(kp_ref, kc_ref)
    v_halves = halves(vp_ref, vc_ref)
    pairs = SWA_GROUP // 2
    nt = (((1,), (1,)), ((), ()))
    for c in range(SWA_KV_HEADS):
        q4 = jnp.concatenate([q_ref[0, :, (c * pairs + p) * LANES:(c * pairs + p + 1) * LANES]
                              for p in range(pairs)], axis=0)
        for par in range(2):
            s_scr[par] = lax.dot_general(q4, k_halves[c][par], nt, preferred_element_type=F32)
        for par in range(2):
            for p in range(pairs):
                sink = sink_ref[c * SWA_GROUP + 2 * p + par]
                rows = slice(p * w, (p + 1) * w)
                s = s_scr[par, rows, :] + bias
                m = jnp.maximum(jnp.max(s, axis=-1, keepdims=True), sink)
                e = jnp.exp(s - m)
                denom = jnp.sum(e, axis=-1, keepdims=True) + jnp.exp(sink - m)
                p_scr[par, rows, :] = (e * (1.0 / denom)).astype(BF16)
        o4 = (jnp.dot(p_scr[0], v_halves[c][0], preferred_element_type=F32)
              + jnp.dot(p_scr[1], v_halves[c][1], preferred_element_type=F32))
        for p in range(pairs):
            o_ref[0, :, (c * pairs + p) * LANES:(c * pairs + p + 1) * LANES] = o4[p * w:(p + 1) * w].astype(BF16)


def _swa(qb, kb, vb, sinks, batch, seq):
    w = WINDOW
    r3 = lambda a: a.reshape(batch, seq, a.shape[-1])
    cur = lambda b, s, sk: (b, s, 0)
    prev = lambda b, s, sk: (b, jnp.maximum(s - 1, 0), 0)
    out = pl.pallas_call(
        _swa_kernel,
        out_shape=jax.ShapeDtypeStruct((batch, seq, SWA_DQ), BF16),
        grid_spec=pltpu.PrefetchScalarGridSpec(
            num_scalar_prefetch=1,
            grid=(batch, seq // w),
            in_specs=[pl.BlockSpec((1, w, SWA_DQ), cur),
                      pl.BlockSpec((1, w, SWA_DKV), prev),
                      pl.BlockSpec((1, w, SWA_DKV), cur),
                      pl.BlockSpec((1, w, SWA_DKV), prev),
                      pl.BlockSpec((1, w, SWA_DKV), cur)],
            out_specs=pl.BlockSpec((1, w, SWA_DQ), cur),
            scratch_shapes=[pltpu.VMEM((2, SWA_GROUP // 2 * w, 2 * w), F32),
                            pltpu.VMEM((2, SWA_GROUP // 2 * w, 2 * w), BF16)]),
        compiler_params=_params(("arbitrary", "arbitrary")),
        name="swa",
    )(sinks, r3(qb), r3(kb), r3(kb), r3(vb), r3(vb))
    return out.reshape(batch * seq, SWA_DQ)


def _top2_sum(a):
    best = a[0] + a[1]
    for i in range(len(a)):
        for j in range(i + 1, len(a)):
            if (i, j) != (0, 1):
                best = jnp.maximum(best, a[i] + a[j])
    return best


_PAIR_SLOTS = ((0, 1), (2, 1), (2, 0), (3, 0), (3, 1), (3, 2))
N_CLASSES = N_GROUPS * len(_PAIR_SLOTS)
CLASS_ROWS = 32
_CLASS_EA = tuple(g * EXPERTS_PER_GROUP + a for g in range(N_GROUPS) for a, _ in _PAIR_SLOTS)
_CLASS_EB = tuple(g * EXPERTS_PER_GROUP + b for g in range(N_GROUPS) for _, b in _PAIR_SLOTS)


def _route(logits_t, rb):
    scores = _sigmoid(logits_t)
    biased = scores + rb
    sc = [scores[e:e + 1, :] for e in range(N_EXPERTS)]
    bi = [biased[e:e + 1, :] for e in range(N_EXPERTS)]
    gs = [_top2_sum(bi[g * EXPERTS_PER_GROUP:(g + 1) * EXPERTS_PER_GROUP]) for g in range(N_GROUPS)]
    best = gs[0]
    for g in range(1, N_GROUPS):
        best = jnp.maximum(best, gs[g])
    taken = jnp.zeros_like(best, dtype=jnp.bool_)
    in_grp = []
    for g in range(N_GROUPS):
        sel = (gs[g] == best) & jnp.logical_not(taken)
        taken = taken | sel
        in_grp.append(sel)
    cand = [jnp.where(in_grp[e // EXPERTS_PER_GROUP], bi[e], -jnp.inf) for e in range(N_EXPERTS)]
    picked = [None] * N_EXPERTS
    chosen = [jnp.zeros_like(taken) for _ in range(N_EXPERTS)]
    for _ in range(2):
        top = cand[0]
        for e in range(1, N_EXPERTS):
            top = jnp.maximum(top, cand[e])
        done = jnp.zeros_like(taken)
        for e in range(N_EXPERTS):
            hit = (cand[e] == top) & jnp.logical_not(done)
            done = done | hit
            chosen[e] = chosen[e] | hit
            cand[e] = jnp.where(hit, -jnp.inf, cand[e])
    total = jnp.zeros_like(best)
    for e in range(N_EXPERTS):
        picked[e] = jnp.where(chosen[e], sc[e], 0.0)
        total = total + picked[e]
    ind = [chosen[_CLASS_EA[k]] & chosen[_CLASS_EB[k]] for k in range(N_CLASSES)]
    w_a = jnp.zeros_like(best)
    w_b = jnp.zeros_like(best)
    for k in range(N_CLASSES):
        w_a = w_a + jnp.where(ind[k], picked[_CLASS_EA[k]], 0.0)
        w_b = w_b + jnp.where(ind[k], picked[_CLASS_EB[k]], 0.0)
    ind_rows = [jnp.where(i, 1.0, 0.0) for i in ind] + [jnp.zeros_like(best)] * (CLASS_ROWS - N_CLASSES)
    return jnp.concatenate(ind_rows, axis=0), w_a / total, w_b / total


def _merge_kernel(x_ref, ya_ref, yb_ref, gta_ref, gtb_ref, wpa_ref, wpb_ref, wo_ref, g1_ref, n2_ref, sc2_ref, sh2_ref,
                  rwt_ref, rb_ref, tri_ref, xo_ref, hx_ref, info_ref, cnt_ref, carry_scr):
    @pl.when(pl.program_id(0) == 0)
    def _():
        carry_scr[...] = jnp.zeros_like(carry_scr)

    pa = jnp.dot(ya_ref[...], wpa_ref[...], preferred_element_type=F32)
    pb = jnp.dot(yb_ref[...], wpb_ref[...], preferred_element_type=F32)
    merged = _sigmoid(gta_ref[...].astype(F32)) * pa + _sigmoid(gtb_ref[...].astype(F32)) * pb
    mix = jnp.dot(merged.astype(BF16), wo_ref[...], preferred_element_type=F32)
    x = x_ref[...] + g1_ref[0] * mix
    xo_ref[...] = x
    h2 = _norm_mod(x, n2_ref[...], sc2_ref[0], sh2_ref[0])
    d = h2.shape[1]
    tm = h2.shape[0]
    hhi, hlo = _split_bf16(h2)
    whi, wlo = _split_bf16(rwt_ref[...])
    nt = (((1,), (1,)), ((), ()))
    logits_t = (lax.dot_general(whi, hhi, nt, preferred_element_type=F32)
                + lax.dot_general(whi, hlo, nt, preferred_element_type=F32)
                + lax.dot_general(wlo, hhi, nt, preferred_element_type=F32))
    ind, w_a, w_b = _route(logits_t, rb_ref[...])
    hx_ref[:, :d] = h2
    hx_ref[:, d:] = jnp.concatenate([w_a, w_b, jnp.zeros((LANES - 2, tm), F32)], axis=0).T
    carry = carry_scr[:, 0:1]
    cum = jnp.dot(ind.astype(BF16), tri_ref[...], preferred_element_type=F32)
    k_col = lax.broadcasted_iota(jnp.int32, (CLASS_ROWS, 1), 0).astype(F32)
    cls = jnp.sum(ind * k_col, axis=0, keepdims=True)
    rank = jnp.sum(ind * (cum + carry), axis=0, keepdims=True) - 1.0
    info = jnp.concatenate([cls, rank, jnp.zeros((6, tm), F32)], axis=0)
    info_ref[0] = info.astype(jnp.int32)
    carry = carry + jnp.sum(ind, axis=1, keepdims=True)
    carry_scr[...] = jnp.broadcast_to(carry, carry_scr.shape)
    cnt_ref[...] = jnp.broadcast_to(carry, cnt_ref.shape).astype(jnp.int32)


def _merge(x2, ya, yb, gta, gtb, wpa, wpb, wo, g1, n2, sc2, sh2, rwt, rb, seq):
    n, d = x2.shape
    tm = TM_PROJ
    per_b = seq // tm
    row = lambda i: (i, 0)
    const = lambda i: (0, 0)
    bsel = lambda i: (i // per_b, 0, 0)
    wspec = pl.BlockSpec((d, d), const, pipeline_mode=pl.Buffered(1))
    t = np.arange(tm)
    tri = jnp.asarray(t[:, None] <= t[None, :], BF16)
    return pl.pallas_call(
        _merge_kernel,
        out_shape=(jax.ShapeDtypeStruct((n, d), F32), jax.ShapeDtypeStruct((n, d + LANES), F32),
                   jax.ShapeDtypeStruct((n // tm, 8, tm), jnp.int32),
                   jax.ShapeDtypeStruct((CLASS_ROWS, LANES), jnp.int32)),
        grid=(n // tm,),
        in_specs=[pl.BlockSpec((tm, d), row)] * 5 + [wspec, wspec, wspec,
                  pl.BlockSpec((1, 1, d), bsel),
                  pl.BlockSpec((1, d), const),
                  pl.BlockSpec((1, 1, d), bsel),
                  pl.BlockSpec((1, 1, d), bsel),
                  pl.BlockSpec((N_EXPERTS, d), const),
                  pl.BlockSpec((N_EXPERTS, 1), const),
                  pl.BlockSpec((tm, tm), const)],
        out_specs=(pl.BlockSpec((tm, d), row), pl.BlockSpec((tm, d + LANES), row),
                   pl.BlockSpec((1, 8, tm), lambda i: (i, 0, 0)),
                   pl.BlockSpec((CLASS_ROWS, LANES), const)),
        scratch_shapes=[pltpu.VMEM((CLASS_ROWS, LANES), F32)],
        compiler_params=_params(("arbitrary",)),
        name="merge_route",
    )(x2, ya, yb, gta, gtb, wpa, wpb, wo, g1, n2, sc2, sh2, rwt, rb, tri)


def _lookup(table, idx):
    hit = idx[:, None] == jnp.arange(table.shape[0], dtype=jnp.int32)[None, :]
    return jnp.sum(jnp.where(hit, table[None, :], 0), axis=1)


def _moe_plan(counts, n):
    tm = TM_MOE
    tiles = n // tm + N_CLASSES
    padded = (counts + tm - 1) // tm * tm
    ends = jnp.cumsum(padded)
    bounds = jnp.concatenate([jnp.zeros((1,), jnp.int32), ends]).astype(jnp.int32)
    nvt = ends[-1] // tm
    pos = jnp.minimum(jnp.arange(tiles, dtype=jnp.int32), nvt - 1) * tm
    k = jnp.sum((ends[None, :] <= pos[:, None]).astype(jnp.int32), axis=1)
    ea = _lookup(jnp.asarray(_CLASS_EA, jnp.int32), k)
    eb = _lookup(jnp.asarray(_CLASS_EB, jnp.int32), k)
    first = jnp.ones((1,), jnp.int32)
    cha = jnp.concatenate([first, (ea[1:] != ea[:-1]).astype(jnp.int32)])
    chb = jnp.concatenate([first, (eb[1:] != eb[:-1]).astype(jnp.int32)])
    return ea, eb, cha, chb, bounds, nvt.reshape(1).astype(jnp.int32)


def _moe_kernel(ea_ref, eb_ref, cha_ref, chb_ref, bnd_ref, cnt_ref, nvt_ref, pos_ref,
                hx_hbm, wga_ref, wua_ref, wda_ref, wgb_ref, wub_ref, wdb_ref, y_ref,
                buf, wg_bf, wu_bf, wd_bf, perm, gsem, *, n_tokens):
    i = pl.program_id(0)
    tm = TM_MOE
    d = y_ref.shape[1]
    nvt = nvt_ref[0]

    def gather_start(tile, slot):
        for r in range(tm):
            pltpu.make_async_copy(hx_hbm.at[pl.ds(perm[tile * tm + r], 1), :], buf.at[slot, pl.ds(r, 1), :],
                                  gsem.at[slot]).start()

    def gather_wait(slot):
        pltpu.make_async_copy(hx_hbm.at[pl.ds(0, tm), :], buf.at[slot], gsem.at[slot]).wait()

    @pl.when(i == 0)
    def _():
        def place(t, carry):
            perm[pos_ref[t]] = t
            return carry

        lax.fori_loop(0, n_tokens, place, 0, unroll=8)

        def pad(p, carry):
            perm[p] = n_tokens - 1
            return carry

        for k in range(N_CLASSES):
            lax.fori_loop(bnd_ref[k] + cnt_ref[k], bnd_ref[k + 1], pad, 0)
        lax.fori_loop(bnd_ref[N_CLASSES], bnd_ref[N_CLASSES] + tm, pad, 0)
        gather_start(0, 0)

    @pl.when((i < nvt) & (cha_ref[i] == 1))
    def _():
        wg_bf[0] = wga_ref[0, 0].astype(BF16)
        wu_bf[0] = wua_ref[0, 0].astype(BF16)
        wd_bf[0] = wda_ref[0, 0].astype(BF16)

    @pl.when((i < nvt) & (chb_ref[i] == 1))
    def _():
        wg_bf[1] = wgb_ref[0, 0].astype(BF16)
        wu_bf[1] = wub_ref[0, 0].astype(BF16)
        wd_bf[1] = wdb_ref[0, 0].astype(BF16)

    @pl.when(i < nvt)
    def _():
        slot = lax.rem(i, 2)
        gather_wait(slot)
        gather_start(i + 1, 1 - slot)
        hx = buf[slot]
        h = hx[:, :d].astype(BF16)
        acc = None
        for s in range(2):
            w = hx[:, d + s:d + s + 1]
            gate = jnp.dot(h, wg_bf[s], preferred_element_type=F32)
            up = jnp.dot(h, wu_bf[s], preferred_element_type=F32)
            hid = (_silu(gate) * up * w).astype(BF16)
            part = jnp.dot(hid, wd_bf[s], preferred_element_type=F32)
            acc = part if acc is None else acc + part
        y_ref[...] = acc

    @pl.when(i >= nvt)
    def _():
        y_ref[...] = jnp.zeros_like(y_ref)

    @pl.when(i == nvt)
    def _():
        gather_wait(lax.rem(nvt, 2))


def _moe(hx, info, cnt, w_gate, w_up, w_down, layer):
    n, dx = hx.shape
    d = dx - LANES
    tm = TM_MOE
    tiles = n // tm + N_CLASSES
    counts = cnt[:, 0]
    ea, eb, cha, chb, bounds, nvt = _moe_plan(counts, n)
    pos = _lookup(bounds, info[:, 0, :].reshape(n)) + info[:, 1, :].reshape(n)
    f = w_gate.shape[-1]
    wa = lambda i, ea, eb, *_: (layer, ea[i], 0, 0)
    wb = lambda i, ea, eb, *_: (layer, eb[i], 0, 0)
    y = pl.pallas_call(
        functools.partial(_moe_kernel, n_tokens=n),
        out_shape=jax.ShapeDtypeStruct((tiles * tm, d), F32),
        grid_spec=pltpu.PrefetchScalarGridSpec(
            num_scalar_prefetch=8,
            grid=(tiles,),
            in_specs=[pl.BlockSpec(memory_space=pl.ANY),
                      pl.BlockSpec((1, 1, d, f), wa), pl.BlockSpec((1, 1, d, f), wa), pl.BlockSpec((1, 1, f, d), wa),
                      pl.BlockSpec((1, 1, d, f), wb), pl.BlockSpec((1, 1, d, f), wb), pl.BlockSpec((1, 1, f, d), wb)],
            out_specs=pl.BlockSpec((tm, d), lambda i, *_: (i, 0)),
            scratch_shapes=[pltpu.VMEM((2, tm, dx), F32),
                            pltpu.VMEM((2, d, f), BF16), pltpu.VMEM((2, d, f), BF16), pltpu.VMEM((2, f, d), BF16),
                            pltpu.SMEM((tiles * tm,), jnp.int32),
                            pltpu.SemaphoreType.DMA((2,))]),
        compiler_params=_params(("arbitrary",)),
        name="moe",
    )(ea, eb, cha, chb, bounds, counts, nvt, pos, hx, w_gate, w_up, w_down, w_gate, w_up, w_down)
    return y, pos


def _final_kernel(pos_ref, x_ref, y_hbm, g2_ref, fin_ref, o_ref, ybuf, ysem):
    rows, finish = _gathered_rows(pos_ref, y_hbm, ybuf, ysem)
    x = x_ref[...] + g2_ref[0] * rows
    ms = jnp.mean(x * x, axis=-1, keepdims=True)
    o_ref[...] = x * lax.rsqrt(ms + EPS) * fin_ref[...]
    finish()


def _final(x2, y, pos, g2, fin, seq):
    n, d = x2.shape
    tm = TM_PROJ
    per_b = seq // tm
    row = lambda i, *_: (i, 0)
    return pl.pallas_call(
        _final_kernel,
        out_shape=jax.ShapeDtypeStruct((n, d), F32),
        grid_spec=pltpu.PrefetchScalarGridSpec(
            num_scalar_prefetch=1,
            grid=(n // tm,),
            in_specs=[pl.BlockSpec((tm, d), row), pl.BlockSpec(memory_space=pl.ANY),
                      pl.BlockSpec((1, 1, d), lambda i, *_: (i // per_b, 0, 0)),
                      pl.BlockSpec((1, d), lambda i, *_: (0, 0))],
            out_specs=pl.BlockSpec((tm, d), row),
            scratch_shapes=[pltpu.VMEM((2, tm, d), F32), pltpu.SemaphoreType.DMA((2,))]),
        compiler_params=_params(("arbitrary",)),
        name="final_norm",
    )(pos, x2, y, g2, fin)


def _pack_w_in(w_in_l):
    parts = jnp.split(w_in_l, [int(i) for i in np.cumsum(
        (GLA_DK, GLA_DK, GLA_DV, GLA_DV, GLA_RANK, SWA_DQ, SWA_DKV, SWA_DKV, D_MODEL))], axis=-1)
    q_a, k_a, v_a, r_a, lr_a, q_b, k_b, v_b, gt_a, gt_b = parts
    lr_pad = jnp.pad(lr_a, ((0, 0), (0, LANES - GLA_RANK)))
    return jnp.concatenate([q_a, k_a, v_a, r_a, q_b, k_b, v_b, gt_a, gt_b, lr_pad], axis=-1).astype(BF16)


def kernel(x, c, positions, ada_w, ada_b, norm1_g, norm2_g, final_g, w_in, gla_alpha_w, gla_alpha_b, gla_norm_g,
           swa_sinks, w_pa, w_pb, w_out, router_w, router_b, moe_w_gate, moe_w_up, moe_w_down):
    batch, seq, d = x.shape
    n = batch * seq
    cos, sin = _rope_tables(positions)
    mod = _modulation(c, ada_w, ada_b)
    rwt = router_w.T
    rb = router_b.reshape(N_EXPERTS, 1)
    x2 = x.reshape(n, d)
    y = pos = g2 = None
    for l in range(DEPTH):
        m = mod[l].reshape(batch, N_MOD, 1, d)
        sh1, sc1, g1, sh2, sc2, g2_l = (m[:, i] for i in range(N_MOD))
        outs = _inproj(x2, y, pos, g2, norm1_g[l][None, :], sc1, sh1, _pack_w_in(w_in[l]), cos, sin, seq)
        if y is not None:
            x2, outs = outs[0], outs[1:]
        qa, ka, va, ra, qb, kb, vb, gta, gtb, lr = outs
        aw = jnp.pad(gla_alpha_w[l], ((0, LANES - GLA_RANK), (0, 0))).astype(BF16)
        ya = _gla(qa, ka, va, ra, lr, aw, gla_alpha_b[l][None, :], gla_norm_g[l][None, :], batch, seq)
        yb = _swa(qb, kb, vb, swa_sinks[l], batch, seq)
        x2, hx, info, cnt = _merge(x2, ya, yb, gta, gtb, w_pa[l].astype(BF16), w_pb[l].astype(BF16),
                                   w_out[l].astype(BF16), g1, norm2_g[l][None, :], sc2, sh2, rwt, rb, seq)
        y, pos = _moe(hx, info, cnt, moe_w_gate, moe_w_up, moe_w_down, l)
        g2 = g2_l
    return _final(x2, y, pos, g2, final_g[None, :], seq).reshape(batch, seq, d)
```

```python
import functools

import jax
import jax.numpy as jnp
import numpy as np
from jax import lax
from jax.experimental import pallas as pl
from jax.experimental.pallas import tpu as pltpu

F32 = jnp.float32
BF16 = jnp.bfloat16

D_MODEL = 1024
DEPTH = 4
GLA_HEADS = 4
GLA_DK = D_MODEL // 2
GLA_DV = D_MODEL
GLA_HK = GLA_DK // GLA_HEADS
GLA_HV = GLA_DV // GLA_HEADS
GLA_RANK = 16
GLA_GATE_NORM = 16.0
GLA_CHUNK = 64
SWA_HEADS = 16
SWA_KV_HEADS = 2
HEAD_DIM = 64
SWA_GROUP = SWA_HEADS // SWA_KV_HEADS
SWA_DQ = SWA_HEADS * HEAD_DIM
SWA_DKV = SWA_KV_HEADS * HEAD_DIM
WINDOW = 128
ROPE_THETA = 10000.0
N_EXPERTS = 16
N_GROUPS = 4
EXPERTS_PER_GROUP = N_EXPERTS // N_GROUPS
D_EXPERT = 512
N_MOD = 6
EPS = 1e-6

LANES = 128
VMEM_LIMIT = 56 * 1024 * 1024
NEG_BIG = -1e30

_W_SEGS = (("qa", GLA_DK), ("ka", GLA_DK), ("va", GLA_DV), ("ra", GLA_DV), ("qb", SWA_DQ), ("kb", SWA_DKV),
           ("vb", SWA_DKV), ("gta", D_MODEL), ("gtb", D_MODEL), ("lr", LANES))
_W_OFF = {}
_o = 0
for _n, _w in _W_SEGS:
    _W_OFF[_n] = (_o, _o + _w)
    _o += _w
D_IN_PAD = _o

TM_PROJ = 512
TG_GLA = 256
TM_MOE = 256
TS_ROPE = 2048


def _params(sem):
    return pltpu.CompilerParams(dimension_semantics=sem, vmem_limit_bytes=VMEM_LIMIT)


def _split_bf16(a):
    hi = a.astype(BF16)
    lo = (a - hi.astype(F32)).astype(BF16)
    return hi, lo


def _sigmoid(x):
    return 1.0 / (1.0 + jnp.exp(-x))


def _silu(x):
    return x * _sigmoid(x)


def _rope_table_kernel(pos_ref, invf_ref, sign_ref, cos_ref, sin_ref):
    ang = pos_ref[...].astype(F32) * invf_ref[...]
    cos_ref[...] = jnp.cos(ang)
    sin_ref[...] = jnp.sin(ang) * sign_ref[...]


def _rope_tables(positions):
    n = positions.size
    half = HEAD_DIM // 2
    inv_freq = jnp.power(ROPE_THETA, -jnp.arange(half, dtype=F32) / half)
    invf = jnp.tile(inv_freq, LANES // half)[None, :]
    sign = jnp.tile(jnp.concatenate([-jnp.ones((half,), F32), jnp.ones((half,), F32)]), LANES // HEAD_DIM)[None, :]
    pos = positions.reshape(n, 1)
    return pl.pallas_call(
        _rope_table_kernel,
        out_shape=(jax.ShapeDtypeStruct((n, LANES), F32), jax.ShapeDtypeStruct((n, LANES), F32)),
        grid=(n // TS_ROPE,),
        in_specs=[pl.BlockSpec((TS_ROPE, 1), lambda i: (i, 0)),
                  pl.BlockSpec((1, LANES), lambda i: (0, 0)),
                  pl.BlockSpec((1, LANES), lambda i: (0, 0))],
        out_specs=(pl.BlockSpec((TS_ROPE, LANES), lambda i: (i, 0)),
                   pl.BlockSpec((TS_ROPE, LANES), lambda i: (i, 0))),
        compiler_params=_params(("arbitrary",)),
        name="rope_tables",
    )(pos, invf, sign)


def _mod_kernel(c_ref, w_ref, b_ref, o_ref):
    cond = _silu(c_ref[...])
    chi, clo = _split_bf16(cond)
    whi, wlo = _split_bf16(w_ref[0])
    acc = jnp.dot(chi, whi, preferred_element_type=F32)
    acc += jnp.dot(chi, wlo, preferred_element_type=F32)
    acc += jnp.dot(clo, whi, preferred_element_type=F32)
    o_ref[0] = acc + b_ref[0]


def _modulation(c, ada_w, ada_b):
    nl, d, n6 = ada_w.shape
    b = c.shape[0]
    tn = 1536
    return pl.pallas_call(
        _mod_kernel,
        out_shape=jax.ShapeDtypeStruct((nl, b, n6), F32),
        grid=(nl, n6 // tn),
        in_specs=[pl.BlockSpec((b, d), lambda l, j: (0, 0)),
                  pl.BlockSpec((1, d, tn), lambda l, j: (l, 0, j)),
                  pl.BlockSpec((1, 1, tn), lambda l, j: (l, 0, j))],
        out_specs=pl.BlockSpec((1, b, tn), lambda l, j: (l, 0, j)),
        compiler_params=_params(("arbitrary", "arbitrary")),
        name="adaln_mod",
    )(c, ada_w, ada_b.reshape(nl, 1, n6))


def _norm_mod(x, g, sc, sh):
    ms = jnp.mean(x * x, axis=-1, keepdims=True)
    return (x * lax.rsqrt(ms + EPS) * g) * (1.0 + sc) + sh


def _rope(t, cos, sin_signed):
    w = t.shape[-1]
    up = pltpu.roll(t, w - HEAD_DIM // 2, 1)
    dn = pltpu.roll(t, HEAD_DIM // 2, 1)
    lane = lax.broadcasted_iota(jnp.int32, t.shape, 1)
    swapped = jnp.where(lane % HEAD_DIM < HEAD_DIM // 2, up, dn)
    reps = w // LANES
    if reps > 1:
        cos = jnp.concatenate([cos] * reps, axis=1)
        sin_signed = jnp.concatenate([sin_signed] * reps, axis=1)
    return t * cos + swapped * sin_signed


def _gathered_rows(pos_ref, y_hbm, ybuf, ysem):
    i = pl.program_id(0)
    steps = pl.num_programs(0)
    tm = ybuf.shape[1]

    def start(step, slot):
        for r in range(tm):
            pltpu.make_async_copy(y_hbm.at[pl.ds(pos_ref[step * tm + r], 1), :], ybuf.at[slot, pl.ds(r, 1), :],
                                  ysem.at[slot]).start()

    def wait(slot):
        pltpu.make_async_copy(y_hbm.at[pl.ds(0, tm), :], ybuf.at[slot], ysem.at[slot]).wait()

    @pl.when(i == 0)
    def _():
        start(0, 0)

    slot = lax.rem(i, 2)
    wait(slot)
    start(lax.rem(i + 1, steps), 1 - slot)

    def finish():
        @pl.when(i == steps - 1)
        def _():
            wait(1 - slot)

    return ybuf[slot], finish


def _inproj_kernel(*refs, has_y):
    finish = None
    if has_y:
        pos_ref, x_ref, y_hbm, g2_ref = refs[:4]
        ybuf, ysem = refs[-2:]
        refs = refs[4:-2]
        rows, finish = _gathered_rows(pos_ref, y_hbm, ybuf, ysem)
        x = x_ref[...] + g2_ref[0] * rows
    else:
        x_ref = refs[0]
        refs = refs[1:]
        x = x_ref[...]
    g_ref, sc_ref, sh_ref, w_ref, cos_ref, sin_ref = refs[:6]
    outs = refs[6:]
    if has_y:
        outs[0][...] = x
        outs = outs[1:]
    qa_ref, ka_ref, va_ref, ra_ref, qb_ref, kb_ref, vb_ref, gta_ref, gtb_ref, lr_ref = outs
    h = _norm_mod(x, g_ref[...], sc_ref[0], sh_ref[0]).astype(BF16)

    def mm(name):
        c0, c1 = _W_OFF[name]
        return jnp.dot(h, w_ref[:, c0:c1], preferred_element_type=F32)

    qa_ref[...] = (mm("qa") * GLA_HK ** -0.5).astype(BF16)
    ka_ref[...] = mm("ka").astype(BF16)
    va_ref[...] = mm("va").astype(BF16)
    ra_ref[...] = mm("ra").astype(BF16)
    cos = cos_ref[...]
    sin = sin_ref[...]
    qb_ref[...] = (_rope(mm("qb"), cos, sin) * HEAD_DIM ** -0.5).astype(BF16)
    kb_ref[...] = _rope(mm("kb"), cos, sin).astype(BF16)
    vb_ref[...] = mm("vb").astype(BF16)
    gta_ref[...] = mm("gta").astype(BF16)
    gtb_ref[...] = mm("gtb").astype(BF16)
    lr_ref[...] = mm("lr")
    if finish is not None:
        finish()


def _inproj(x2, y, pos, g2, g, sc, sh, w, cos, sin, seq):
    n, d = x2.shape
    tm = TM_PROJ
    per_b = seq // tm
    row = lambda i, *_: (i, 0)
    const = lambda i, *_: (0, 0)
    bsel = lambda i, *_: (i // per_b, 0, 0)
    has_y = y is not None
    widths = [(wd, BF16) for _, wd in _W_SEGS[:-1]] + [(LANES, F32)]
    if has_y:
        widths = [(d, F32)] + widths
    resid_specs = [pl.BlockSpec(memory_space=pl.ANY), pl.BlockSpec((1, 1, d), bsel)] if has_y else []
    args = (pos, x2, y, g2) if has_y else (x2,)
    scratch = [pltpu.VMEM((2, tm, d), F32), pltpu.SemaphoreType.DMA((2,))] if has_y else []
    return pl.pallas_call(
        functools.partial(_inproj_kernel, has_y=has_y),
        out_shape=tuple(jax.ShapeDtypeStruct((n, wd), dt) for wd, dt in widths),
        grid_spec=pltpu.PrefetchScalarGridSpec(
            num_scalar_prefetch=1 if has_y else 0,
            grid=(n // tm,),
            in_specs=[pl.BlockSpec((tm, d), row)] + resid_specs + [
                      pl.BlockSpec((1, d), const),
                      pl.BlockSpec((1, 1, d), bsel),
                      pl.BlockSpec((1, 1, d), bsel),
                      pl.BlockSpec((d, D_IN_PAD), const, pipeline_mode=pl.Buffered(1)),
                      pl.BlockSpec((tm, LANES), row),
                      pl.BlockSpec((tm, LANES), row)],
            out_specs=tuple(pl.BlockSpec((tm, wd), row) for wd, _ in widths),
            scratch_shapes=scratch),
        compiler_params=_params(("arbitrary",)),
        name="norm_inproj",
    )(*args, g, sc, sh, w, cos, sin)


def _gla_kernel(q_ref, k_ref, v_ref, r_ref, lr_ref, aw_ref, ab_ref, gn_ref, tri_ref, o_ref, b_scr, st_scr):
    tg = q_ref.shape[1]
    c = GLA_CHUNK

    @pl.when(pl.program_id(1) == 0)
    def _():
        st_scr[...] = jnp.zeros_like(st_scr)

    pre = jnp.dot(lr_ref[0].astype(BF16), aw_ref[...], preferred_element_type=F32) + ab_ref[...]
    g = (jnp.minimum(pre, 0.0) - jnp.log1p(jnp.exp(-jnp.abs(pre)))) * (1.0 / GLA_GATE_NORM)
    ghi, glo = _split_bf16(g)
    tri = tri_ref[...]
    b_scr[...] = jnp.dot(tri, ghi, preferred_element_type=F32) + jnp.dot(tri, glo, preferred_element_type=F32)

    row = lax.broadcasted_iota(jnp.int32, (c, c), 0)
    col = lax.broadcasted_iota(jnp.int32, (c, c), 1)
    causal = col <= row
    gn = gn_ref[...]

    def chunk(ci, carry):
        r0 = pl.multiple_of(ci * c, c)
        rows = pl.ds(r0, c)
        for hh in range(GLA_HEADS):
            kc = slice(hh * GLA_HK, (hh + 1) * GLA_HK)
            vc = slice(hh * GLA_HV, (hh + 1) * GLA_HV)
            b = b_scr[rows, kc]
            b_last = b[c - 1:c, :]
            q = q_ref[0, rows, kc].astype(F32)
            k = k_ref[0, rows, kc].astype(F32)
            v = v_ref[0, rows, vc]
            q_e = (q * jnp.exp(b)).astype(BF16)
            k_i = (k * jnp.exp(-b)).astype(BF16)
            k_s = (k * jnp.exp(b_last - b)).astype(BF16)
            att = lax.dot_general(q_e, k_i, (((1,), (1,)), ((), ())), preferred_element_type=F32)
            att = jnp.where(causal, att, 0.0).astype(BF16)
            st = st_scr[hh]
            o = jnp.dot(att, v, preferred_element_type=F32)
            o += lax.dot_general(q_e, st.astype(BF16), (((1,), (1,)), ((), ())), preferred_element_type=F32)
            upd = lax.dot_general(v, k_s, (((0,), (0,)), ((), ())), preferred_element_type=F32)
            st_scr[hh] = st * jnp.exp(b_last) + upd
            ms = jnp.mean(o * o, axis=-1, keepdims=True)
            y = o * lax.rsqrt(ms + EPS) * gn
            r = r_ref[0, rows, vc].astype(F32)
            o_ref[0, rows, vc] = (y * _silu(r)).astype(BF16)
        return carry

    lax.fori_loop(0, tg // c, chunk, 0, unroll=True)


def _gla(qa, ka, va, ra, lr, aw, ab, gn, batch, seq):
    tg = TG_GLA
    blk = lambda b, s: (b, s, 0)
    const = lambda b, s: (0, 0)
    t = np.arange(tg)
    tri = jnp.asarray(((t[:, None] // GLA_CHUNK == t[None, :] // GLA_CHUNK) & (t[None, :] <= t[:, None])), BF16)
    r3 = lambda a: a.reshape(batch, seq, a.shape[-1])
    out = pl.pallas_call(
        _gla_kernel,
        out_shape=jax.ShapeDtypeStruct((batch, seq, GLA_DV), BF16),
        grid=(batch, seq // tg),
        in_specs=[pl.BlockSpec((1, tg, GLA_DK), blk),
                  pl.BlockSpec((1, tg, GLA_DK), blk),
                  pl.BlockSpec((1, tg, GLA_DV), blk),
                  pl.BlockSpec((1, tg, GLA_DV), blk),
                  pl.BlockSpec((1, tg, LANES), blk),
                  pl.BlockSpec((LANES, GLA_DK), const),
                  pl.BlockSpec((1, GLA_DK), const),
                  pl.BlockSpec((1, GLA_HV), const),
                  pl.BlockSpec((tg, tg), const)],
        out_specs=pl.BlockSpec((1, tg, GLA_DV), blk),
        scratch_shapes=[pltpu.VMEM((tg, GLA_DK), F32), pltpu.VMEM((GLA_HEADS, GLA_HV, GLA_HK), F32)],
        compiler_params=_params(("arbitrary", "arbitrary")),
        name="gla",
    )(r3(qa), r3(ka), r3(va), r3(ra), r3(lr), aw, ab, gn, tri)
    return out.reshape(batch * seq, GLA_DV)


def _swa_kernel(sink_ref, q_ref, kp_ref, kc_ref, vp_ref, vc_ref, o_ref, s_scr, p_scr):
    w = WINDOW
    blk = pl.program_id(1)
    t = lax.broadcasted_iota(jnp.int32, (w, 2 * w), 0)
    j = lax.broadcasted_iota(jnp.int32, (w, 2 * w), 1)
    diff = w + t - j
    valid = (diff >= 0) & (diff < w) & (blk * w + j - w >= 0)
    bias = jnp.where(valid, 0.0, NEG_BIG)

    lane = lax.broadcasted_iota(jnp.int32, (2 * w, LANES), 1)
    low = lane < HEAD_DIM

    def halves(prev_ref, cur_ref):
        both = jnp.concatenate([prev_ref[0], cur_ref[0]], axis=0)
        swapped = pltpu.roll(both, HEAD_DIM, 1)
        zero = jnp.zeros_like(both)
        kv0 = (jnp.where(low, both, zero), jnp.where(low, zero, swapped))
        kv1 = (jnp.where(low, swapped, zero), jnp.where(low, zero, both))
        return kv0, kv1

    k_halves = halves(kp_ref, kc_ref)
    v_halves = halves(vp_ref, vc_ref)
    pairs = SWA_GROUP // 2
    nt = (((1,), (1,)), ((), ()))
    for c in range(SWA_KV_HEADS):
        q4 = jnp.concatenate([q_ref[0, :, (c * pairs + p) * LANES:(c * pairs + p + 1) * LANES]
                              for p in range(pairs)], axis=0)
        for par in range(2):
            s_scr[par] = lax.dot_general(q4, k_halves[c][par], nt, preferred_element_type=F32)
        for par in range(2):
            for p in range(pairs):
                sink = sink_ref[c * SWA_GROUP + 2 * p + par]
                rows = slice(p * w, (p + 1) * w)
                s = s_scr[par, rows, :] + bias
                m = jnp.maximum(jnp.max(s, axis=-1, keepdims=True), sink)
                e = jnp.exp(s - m)
                denom = jnp.sum(e, axis=-1, keepdims=True) + jnp.exp(sink - m)
                p_scr[par, rows, :] = (e * (1.0 / denom)).astype(BF16)
        o4 = (jnp.dot(p_scr[0], v_halves[c][0], preferred_element_type=F32)
              + jnp.dot(p_scr[1], v_halves[c][1], preferred_element_type=F32))
        for p in range(pairs):
            o_ref[0, :, (c * pairs + p) * LANES:(c * pairs + p + 1) * LANES] = o4[p * w:(p + 1) * w].astype(BF16)


def _swa(qb, kb, vb, sinks, batch, seq):
    w = WINDOW
    r3 = lambda a: a.reshape(batch, seq, a.shape[-1])
    cur = lambda b, s, sk: (b, s, 0)
    prev = lambda b, s, sk: (b, jnp.maximum(s - 1, 0), 0)
    out = pl.pallas_call(
        _swa_kernel,
        out_shape=jax.ShapeDtypeStruct((batch, seq, SWA_DQ), BF16),
        grid_spec=pltpu.PrefetchScalarGridSpec(
            num_scalar_prefetch=1,
            grid=(batch, seq // w),
            in_specs=[pl.BlockSpec((1, w, SWA_DQ), cur),
                      pl.BlockSpec((1, w, SWA_DKV), prev),
                      pl.BlockSpec((1, w, SWA_DKV), cur),
                      pl.BlockSpec((1, w, SWA_DKV), prev),
                      pl.BlockSpec((1, w, SWA_DKV), cur)],
            out_specs=pl.BlockSpec((1, w, SWA_DQ), cur),
            scratch_shapes=[pltpu.VMEM((2, SWA_GROUP // 2 * w, 2 * w), F32),
                            pltpu.VMEM((2, SWA_GROUP // 2 * w, 2 * w), BF16)]),
        compiler_params=_params(("arbitrary", "arbitrary")),
        name="swa",
    )(sinks, r3(qb), r3(kb), r3(kb), r3(vb), r3(vb))
    return out.reshape(batch * seq, SWA_DQ)


def _top2_sum(a):
    best = a[0] + a[1]
    for i in range(len(a)):
        for j in range(i + 1, len(a)):
            if (i, j) != (0, 1):
                best = jnp.maximum(best, a[i] + a[j])
    return best


_PAIR_SLOTS = ((0, 1), (2, 1), (2, 0), (3, 0), (3, 1), (3, 2))
N_CLASSES = N_GROUPS * len(_PAIR_SLOTS)
CLASS_ROWS = 32
HX_ROWS = D_MODEL // LANES + 1
_CLASS_EA = tuple(g * EXPERTS_PER_GROUP + a for g in range(N_GROUPS) for a, _ in _PAIR_SLOTS)
_CLASS_EB = tuple(g * EXPERTS_PER_GROUP + b for g in range(N_GROUPS) for _, b in _PAIR_SLOTS)


def _route(logits_t, rb):
    scores = _sigmoid(logits_t)
    biased = scores + rb
    sc = [scores[e:e + 1, :] for e in range(N_EXPERTS)]
    bi = [biased[e:e + 1, :] for e in range(N_EXPERTS)]
    gs = [_top2_sum(bi[g * EXPERTS_PER_GROUP:(g + 1) * EXPERTS_PER_GROUP]) for g in range(N_GROUPS)]
    best = gs[0]
    for g in range(1, N_GROUPS):
        best = jnp.maximum(best, gs[g])
    taken = jnp.zeros_like(best, dtype=jnp.bool_)
    in_grp = []
    for g in range(N_GROUPS):
        sel = (gs[g] == best) & jnp.logical_not(taken)
        taken = taken | sel
        in_grp.append(sel)
    cand = [jnp.where(in_grp[e // EXPERTS_PER_GROUP], bi[e], -jnp.inf) for e in range(N_EXPERTS)]
    picked = [None] * N_EXPERTS
    chosen = [jnp.zeros_like(taken) for _ in range(N_EXPERTS)]
    for _ in range(2):
        top = cand[0]
        for e in range(1, N_EXPERTS):
            top = jnp.maximum(top, cand[e])
        done = jnp.zeros_like(taken)
        for e in range(N_EXPERTS):
            hit = (cand[e] == top) & jnp.logical_not(done)
            done = done | hit
            chosen[e] = chosen[e] | hit
            cand[e] = jnp.where(hit, -jnp.inf, cand[e])
    total = jnp.zeros_like(best)
    for e in range(N_EXPERTS):
        picked[e] = jnp.where(chosen[e], sc[e], 0.0)
        total = total + picked[e]
    ind = [chosen[_CLASS_EA[k]] & chosen[_CLASS_EB[k]] for k in range(N_CLASSES)]
    w_a = jnp.zeros_like(best)
    w_b = jnp.zeros_like(best)
    for k in range(N_CLASSES):
        w_a = w_a + jnp.where(ind[k], picked[_CLASS_EA[k]], 0.0)
        w_b = w_b + jnp.where(ind[k], picked[_CLASS_EB[k]], 0.0)
    ind_rows = [jnp.where(i, 1.0, 0.0) for i in ind] + [jnp.zeros_like(best)] * (CLASS_ROWS - N_CLASSES)
    return jnp.concatenate(ind_rows, axis=0), w_a / total, w_b / total


def _merge_kernel(x_ref, ya_ref, yb_ref, gta_ref, gtb_ref, wpa_ref, wpb_ref, wo_ref, g1_ref, n2_ref, sc2_ref, sh2_ref,
                  rwt_ref, rb_ref, tri_ref, xo_ref, hx_ref, info_ref, cnt_ref, carry_scr):
    @pl.when(pl.program_id(0) == 0)
    def _():
        carry_scr[...] = jnp.zeros_like(carry_scr)

    pa = jnp.dot(ya_ref[...], wpa_ref[...], preferred_element_type=F32)
    pb = jnp.dot(yb_ref[...], wpb_ref[...], preferred_element_type=F32)
    merged = _sigmoid(gta_ref[...].astype(F32)) * pa + _sigmoid(gtb_ref[...].astype(F32)) * pb
    mix = jnp.dot(merged.astype(BF16), wo_ref[...], preferred_element_type=F32)
    x = x_ref[...] + g1_ref[0] * mix
    xo_ref[...] = x
    h2 = _norm_mod(x, n2_ref[...], sc2_ref[0], sh2_ref[0])
    d = h2.shape[1]
    tm = h2.shape[0]
    hhi, hlo = _split_bf16(h2)
    whi, wlo = _split_bf16(rwt_ref[...])
    nt = (((1,), (1,)), ((), ()))
    logits_t = (lax.dot_general(whi, hhi, nt, preferred_element_type=F32)
                + lax.dot_general(whi, hlo, nt, preferred_element_type=F32)
                + lax.dot_general(wlo, hhi, nt, preferred_element_type=F32))
    ind, w_a, w_b = _route(logits_t, rb_ref[...])
    for k in range(d // LANES):
        hx_ref[:, k, :] = h2[:, k * LANES:(k + 1) * LANES]
    hx_ref[:, d // LANES, :] = jnp.concatenate([w_a, w_b, jnp.zeros((LANES - 2, tm), F32)], axis=0).T
    carry = carry_scr[:, 0:1]
    cum = jnp.dot(ind.astype(BF16), tri_ref[...], preferred_element_type=F32)
    k_col = lax.broadcasted_iota(jnp.int32, (CLASS_ROWS, 1), 0).astype(F32)
    cls = jnp.sum(ind * k_col, axis=0, keepdims=True)
    rank = jnp.sum(ind * (cum + carry), axis=0, keepdims=True) - 1.0
    info = jnp.concatenate([cls, rank, jnp.zeros((6, tm), F32)], axis=0)
    info_ref[0] = info.astype(jnp.int32)
    carry = carry + jnp.sum(ind, axis=1, keepdims=True)
    carry_scr[...] = jnp.broadcast_to(carry, carry_scr.shape)
    cnt_ref[...] = jnp.broadcast_to(carry, cnt_ref.shape).astype(jnp.int32)


def _merge(x2, ya, yb, gta, gtb, wpa, wpb, wo, g1, n2, sc2, sh2, rwt, rb, seq):
    n, d = x2.shape
    tm = TM_PROJ
    per_b = seq // tm
    row = lambda i: (i, 0)
    const = lambda i: (0, 0)
    bsel = lambda i: (i // per_b, 0, 0)
    wspec = pl.BlockSpec((d, d), const, pipeline_mode=pl.Buffered(1))
    t = np.arange(tm)
    tri = jnp.asarray(t[:, None] <= t[None, :], BF16)
    return pl.pallas_call(
        _merge_kernel,
        out_shape=(jax.ShapeDtypeStruct((n, d), F32), jax.ShapeDtypeStruct((n, HX_ROWS, LANES), F32),
                   jax.ShapeDtypeStruct((n // tm, 8, tm), jnp.int32),
                   jax.ShapeDtypeStruct((CLASS_ROWS, LANES), jnp.int32)),
        grid=(n // tm,),
        in_specs=[pl.BlockSpec((tm, d), row)] * 5 + [wspec, wspec, wspec,
                  pl.BlockSpec((1, 1, d), bsel),
                  pl.BlockSpec((1, d), const),
                  pl.BlockSpec((1, 1, d), bsel),
                  pl.BlockSpec((1, 1, d), bsel),
                  pl.BlockSpec((N_EXPERTS, d), const),
                  pl.BlockSpec((N_EXPERTS, 1), const),
                  pl.BlockSpec((tm, tm), const)],
        out_specs=(pl.BlockSpec((tm, d), row), pl.BlockSpec((tm, HX_ROWS, LANES), lambda i: (i, 0, 0)),
                   pl.BlockSpec((1, 8, tm), lambda i: (i, 0, 0)),
                   pl.BlockSpec((CLASS_ROWS, LANES), const)),
        scratch_shapes=[pltpu.VMEM((CLASS_ROWS, LANES), F32)],
        compiler_params=_params(("arbitrary",)),
        name="merge_route",
    )(x2, ya, yb, gta, gtb, wpa, wpb, wo, g1, n2, sc2, sh2, rwt, rb, tri)


def _lookup(table, idx):
    hit = idx[:, None] == jnp.arange(table.shape[0], dtype=jnp.int32)[None, :]
    return jnp.sum(jnp.where(hit, table[None, :], 0), axis=1)


def _moe_plan(counts, n):
    tm = TM_MOE
    tiles = n // tm + N_CLASSES
    padded = (counts + tm - 1) // tm * tm
    ends = jnp.cumsum(padded)
    bounds = jnp.concatenate([jnp.zeros((1,), jnp.int32), ends]).astype(jnp.int32)
    nvt = ends[-1] // tm
    pos = jnp.minimum(jnp.arange(tiles, dtype=jnp.int32), nvt - 1) * tm
    k = jnp.sum((ends[None, :] <= pos[:, None]).astype(jnp.int32), axis=1)
    ea = _lookup(jnp.asarray(_CLASS_EA, jnp.int32), k)
    eb = _lookup(jnp.asarray(_CLASS_EB, jnp.int32), k)
    first = jnp.ones((1,), jnp.int32)
    cha = jnp.concatenate([first, (ea[1:] != ea[:-1]).astype(jnp.int32)])
    chb = jnp.concatenate([first, (eb[1:] != eb[:-1]).astype(jnp.int32)])
    return ea, eb, cha, chb, bounds, nvt.reshape(1).astype(jnp.int32)


def _moe_kernel(ea_ref, eb_ref, cha_ref, chb_ref, bnd_ref, cnt_ref, nvt_ref, pos_ref,
                hx_hbm, wga_ref, wua_ref, wda_ref, wgb_ref, wub_ref, wdb_ref, y_ref,
                buf, wg_bf, wu_bf, wd_bf, perm, gsem, *, n_tokens):
    i = pl.program_id(0)
    tm = TM_MOE
    d = y_ref.shape[1]
    nvt = nvt_ref[0]

    def gather_start(tile, slot):
        for r in range(tm):
            pltpu.make_async_copy(hx_hbm.at[perm[tile * tm + r]], buf.at[slot, r], gsem.at[slot]).start()

    def gather_wait(slot):
        pltpu.make_async_copy(hx_hbm.at[pl.ds(0, tm)], buf.at[slot], gsem.at[slot]).wait()

    @pl.when(i == 0)
    def _():
        def place(t, carry):
            perm[pos_ref[t]] = t
            return carry

        lax.fori_loop(0, n_tokens, place, 0, unroll=8)

        def pad(p, carry):
            perm[p] = n_tokens - 1
            return carry

        for k in range(N_CLASSES):
            lax.fori_loop(bnd_ref[k] + cnt_ref[k], bnd_ref[k + 1], pad, 0)
        lax.fori_loop(bnd_ref[N_CLASSES], bnd_ref[N_CLASSES] + tm, pad, 0)
        gather_start(0, 0)

    @pl.when((i < nvt) & (cha_ref[i] == 1))
    def _():
        wg_bf[0] = wga_ref[0, 0].astype(BF16)
        wu_bf[0] = wua_ref[0, 0].astype(BF16)
        wd_bf[0] = wda_ref[0, 0].astype(BF16)

    @pl.when((i < nvt) & (chb_ref[i] == 1))
    def _():
        wg_bf[1] = wgb_ref[0, 0].astype(BF16)
        wu_bf[1] = wub_ref[0, 0].astype(BF16)
        wd_bf[1] = wdb_ref[0, 0].astype(BF16)

    @pl.when(i < nvt)
    def _():
        slot = lax.rem(i, 2)
        gather_wait(slot)
        gather_start(i + 1, 1 - slot)
        tok = buf.at[slot]
        h = jnp.concatenate([tok[:, k, :] for k in range(d // LANES)], axis=1).astype(BF16)
        w_ab = tok[:, d // LANES, :]
        acc = None
        for s in range(2):
            w = w_ab[:, s:s + 1]
            gate = jnp.dot(h, wg_bf[s], preferred_element_type=F32)
            up = jnp.dot(h, wu_bf[s], preferred_element_type=F32)
            hid = (_silu(gate) * up * w).astype(BF16)
            part = jnp.dot(hid, wd_bf[s], preferred_element_type=F32)
            acc = part if acc is None else acc + part
        y_ref[...] = acc

    @pl.when(i >= nvt)
    def _():
        y_ref[...] = jnp.zeros_like(y_ref)

    @pl.when(i == nvt)
    def _():
        gather_wait(lax.rem(nvt, 2))


def _moe(hx, info, cnt, w_gate, w_up, w_down, layer):
    n = hx.shape[0]
    d = w_gate.shape[-2]
    tm = TM_MOE
    tiles = n // tm + N_CLASSES
    counts = cnt[:, 0]
    ea, eb, cha, chb, bounds, nvt = _moe_plan(counts, n)
    pos = _lookup(bounds, info[:, 0, :].reshape(n)) + info[:, 1, :].reshape(n)
    f = w_gate.shape[-1]
    wa = lambda i, ea, eb, *_: (layer, ea[i], 0, 0)
    wb = lambda i, ea, eb, *_: (layer, eb[i], 0, 0)
    y = pl.pallas_call(
        functools.partial(_moe_kernel, n_tokens=n),
        out_shape=jax.ShapeDtypeStruct((tiles * tm, d), F32),
        grid_spec=pltpu.PrefetchScalarGridSpec(
            num_scalar_prefetch=8,
            grid=(tiles,),
            in_specs=[pl.BlockSpec(memory_space=pl.ANY),
                      pl.BlockSpec((1, 1, d, f), wa), pl.BlockSpec((1, 1, d, f), wa), pl.BlockSpec((1, 1, f, d), wa),
                      pl.BlockSpec((1, 1, d, f), wb), pl.BlockSpec((1, 1, d, f), wb), pl.BlockSpec((1, 1, f, d), wb)],
            out_specs=pl.BlockSpec((tm, d), lambda i, *_: (i, 0)),
            scratch_shapes=[pltpu.VMEM((2, tm, HX_ROWS, LANES), F32),
                            pltpu.VMEM((2, d, f), BF16), pltpu.VMEM((2, d, f), BF16), pltpu.VMEM((2, f, d), BF16),
                            pltpu.SMEM((tiles * tm,), jnp.int32),
                            pltpu.SemaphoreType.DMA((2,))]),
        compiler_params=_params(("arbitrary",)),
        name="moe",
    )(ea, eb, cha, chb, bounds, counts, nvt, pos, hx, w_gate, w_up, w_down, w_gate, w_up, w_down)
    return y, pos


def _final_kernel(pos_ref, x_ref, y_hbm, g2_ref, fin_ref, o_ref, ybuf, ysem):
    rows, finish = _gathered_rows(pos_ref, y_hbm, ybuf, ysem)
    x = x_ref[...] + g2_ref[0] * rows
    ms = jnp.mean(x * x, axis=-1, keepdims=True)
    o_ref[...] = x * lax.rsqrt(ms + EPS) * fin_ref[...]
    finish()


def _final(x2, y, pos, g2, fin, seq):
    n, d = x2.shape
    tm = TM_PROJ
    per_b = seq // tm
    row = lambda i, *_: (i, 0)
    return pl.pallas_call(
        _final_kernel,
        out_shape=jax.ShapeDtypeStruct((n, d), F32),
        grid_spec=pltpu.PrefetchScalarGridSpec(
            num_scalar_prefetch=1,
            grid=(n // tm,),
            in_specs=[pl.BlockSpec((tm, d), row), pl.BlockSpec(memory_space=pl.ANY),
                      pl.BlockSpec((1, 1, d), lambda i, *_: (i // per_b, 0, 0)),
                      pl.BlockSpec((1, d), lambda i, *_: (0, 0))],
            out_specs=pl.BlockSpec((tm, d), row),
            scratch_shapes=[pltpu.VMEM((2, tm, d), F32), pltpu.SemaphoreType.DMA((2,))]),
        compiler_params=_params(("arbitrary",)),
        name="final_norm",
    )(pos, x2, y, g2, fin)


def _pack_w_in(w_in_l):
    parts = jnp.split(w_in_l, [int(i) for i in np.cumsum(
        (GLA_DK, GLA_DK, GLA_DV, GLA_DV, GLA_RANK, SWA_DQ, SWA_DKV, SWA_DKV, D_MODEL))], axis=-1)
    q_a, k_a, v_a, r_a, lr_a, q_b, k_b, v_b, gt_a, gt_b = parts
    lr_pad = jnp.pad(lr_a, ((0, 0), (0, LANES - GLA_RANK)))
    return jnp.concatenate([q_a, k_a, v_a, r_a, q_b, k_b, v_b, gt_a, gt_b, lr_pad], axis=-1).astype(BF16)


def kernel(x, c, positions, ada_w, ada_b, norm1_g, norm2_g, final_g, w_in, gla_alpha_w, gla_alpha_b, gla_norm_g,
           swa_sinks, w_pa, w_pb, w_out, router_w, router_b, moe_w_gate, moe_w_up, moe_w_down):
    batch, seq, d = x.shape
    n = batch * seq
    cos, sin = _rope_tables(positions)
    mod = _modulation(c, ada_w, ada_b)
    rwt = router_w.T
    rb = router_b.reshape(N_EXPERTS, 1)
    x2 = x.reshape(n, d)
    y = pos = g2 = None
    for l in range(DEPTH):
        m = mod[l].reshape(batch, N_MOD, 1, d)
        sh1, sc1, g1, sh2, sc2, g2_l = (m[:, i] for i in range(N_MOD))
        outs = _inproj(x2, y, pos, g2, norm1_g[l][None, :], sc1, sh1, _pack_w_in(w_in[l]), cos, sin, seq)
        if y is not None:
            x2, outs = outs[0], outs[1:]
        qa, ka, va, ra, qb, kb, vb, gta, gtb, lr = outs
        aw = jnp.pad(gla_alpha_w[l], ((0, LANES - GLA_RANK), (0, 0))).astype(BF16)
        ya = _gla(qa, ka, va, ra, lr, aw, gla_alpha_b[l][None, :], gla_norm_g[l][None, :], batch, seq)
        yb = _swa(qb, kb, vb, swa_sinks[l], batch, seq)
        x2, hx, info, cnt = _merge(x2, ya, yb, gta, gtb, w_pa[l].astype(BF16), w_pb[l].astype(BF16),
                                   w_out[l].astype(BF16), g1, norm2_g[l][None, :], sc2, sh2, rwt, rb, seq)
        y, pos = _moe(hx, info, cnt, moe_w_gate, moe_w_up, moe_w_down, l)
        g2 = g2_l
    return _final(x2, y, pos, g2, final_g[None, :], seq).reshape(batch, seq, d)
```

```python
import functools

import jax
import jax.numpy as jnp
import numpy as np
from jax import lax
from jax.experimental import pallas as pl
from jax.experimental.pallas import tpu as pltpu

F32 = jnp.float32
BF16 = jnp.bfloat16

D_MODEL = 1024
DEPTH = 4
GLA_HEADS = 4
GLA_DK = D_MODEL // 2
GLA_DV = D_MODEL
GLA_HK = GLA_DK // GLA_HEADS
GLA_HV = GLA_DV // GLA_HEADS
GLA_RANK = 16
GLA_GATE_NORM = 16.0
GLA_CHUNK = 64
SWA_HEADS = 16
SWA_KV_HEADS = 2
HEAD_DIM = 64
SWA_GROUP = SWA_HEADS // SWA_KV_HEADS
SWA_DQ = SWA_HEADS * HEAD_DIM
SWA_DKV = SWA_KV_HEADS * HEAD_DIM
WINDOW = 128
ROPE_THETA = 10000.0
N_EXPERTS = 16
N_GROUPS = 4
EXPERTS_PER_GROUP = N_EXPERTS // N_GROUPS
D_EXPERT = 512
N_MOD = 6
EPS = 1e-6

LANES = 128
VMEM_LIMIT = 56 * 1024 * 1024
NEG_BIG = -1e30

_W_SEGS = (("qa", GLA_DK), ("ka", GLA_DK), ("va", GLA_DV), ("ra", GLA_DV), ("qb", SWA_DQ), ("kb", SWA_DKV),
           ("vb", SWA_DKV), ("gta", D_MODEL), ("gtb", D_MODEL), ("lr", LANES))
_W_OFF = {}
_o = 0
for _n, _w in _W_SEGS:
    _W_OFF[_n] = (_o, _o + _w)
    _o += _w
D_IN_PAD = _o

TM_PROJ = 512
TG_GLA = 256
TM_MOE = 256
TS_ROPE = 2048


def _params(sem):
    return pltpu.CompilerParams(dimension_semantics=sem, vmem_limit_bytes=VMEM_LIMIT)


def _split_bf16(a):
    hi = a.astype(BF16)
    lo = (a - hi.astype(F32)).astype(BF16)
    return hi, lo


def _sigmoid(x):
    return 1.0 / (1.0 + jnp.exp(-x))


def _silu(x):
    return x * _sigmoid(x)


def _rope_table_kernel(pos_ref, invf_ref, sign_ref, cos_ref, sin_ref):
    ang = pos_ref[...].astype(F32) * invf_ref[...]
    cos_ref[...] = jnp.cos(ang)
    sin_ref[...] = jnp.sin(ang) * sign_ref[...]


def _rope_tables(positions):
    n = positions.size
    half = HEAD_DIM // 2
    inv_freq = jnp.power(ROPE_THETA, -jnp.arange(half, dtype=F32) / half)
    invf = jnp.tile(inv_freq, LANES // half)[None, :]
    sign = jnp.tile(jnp.concatenate([-jnp.ones((half,), F32), jnp.ones((half,), F32)]), LANES // HEAD_DIM)[None, :]
    pos = positions.reshape(n, 1)
    return pl.pallas_call(
        _rope_table_kernel,
        out_shape=(jax.ShapeDtypeStruct((n, LANES), F32), jax.ShapeDtypeStruct((n, LANES), F32)),
        grid=(n // TS_ROPE,),
        in_specs=[pl.BlockSpec((TS_ROPE, 1), lambda i: (i, 0)),
                  pl.BlockSpec((1, LANES), lambda i: (0, 0)),
                  pl.BlockSpec((1, LANES), lambda i: (0, 0))],
        out_specs=(pl.BlockSpec((TS_ROPE, LANES), lambda i: (i, 0)),
                   pl.BlockSpec((TS_ROPE, LANES), lambda i: (i, 0))),
        compiler_params=_params(("arbitrary",)),
        name="rope_tables",
    )(pos, invf, sign)


def _mod_kernel(c_ref, w_ref, b_ref, o_ref):
    cond = _silu(c_ref[...])
    chi, clo = _split_bf16(cond)
    whi, wlo = _split_bf16(w_ref[0])
    acc = jnp.dot(chi, whi, preferred_element_type=F32)
    acc += jnp.dot(chi, wlo, preferred_element_type=F32)
    acc += jnp.dot(clo, whi, preferred_element_type=F32)
    o_ref[0] = acc + b_ref[0]


def _modulation(c, ada_w, ada_b):
    nl, d, n6 = ada_w.shape
    b = c.shape[0]
    tn = 1536
    return pl.pallas_call(
        _mod_kernel,
        out_shape=jax.ShapeDtypeStruct((nl, b, n6), F32),
        grid=(nl, n6 // tn),
        in_specs=[pl.BlockSpec((b, d), lambda l, j: (0, 0)),
                  pl.BlockSpec((1, d, tn), lambda l, j: (l, 0, j)),
                  pl.BlockSpec((1, 1, tn), lambda l, j: (l, 0, j))],
        out_specs=pl.BlockSpec((1, b, tn), lambda l, j: (l, 0, j)),
        compiler_params=_params(("arbitrary", "arbitrary")),
        name="adaln_mod",
    )(c, ada_w, ada_b.reshape(nl, 1, n6))


def _norm_mod(x, g, sc, sh):
    ms = jnp.mean(x * x, axis=-1, keepdims=True)
    return (x * lax.rsqrt(ms + EPS) * g) * (1.0 + sc) + sh


def _rope(t, cos, sin_signed):
    w = t.shape[-1]
    up = pltpu.roll(t, w - HEAD_DIM // 2, 1)
    dn = pltpu.roll(t, HEAD_DIM // 2, 1)
    lane = lax.broadcasted_iota(jnp.int32, t.shape, 1)
    swapped = jnp.where(lane % HEAD_DIM < HEAD_DIM // 2, up, dn)
    reps = w // LANES
    if reps > 1:
        cos = jnp.concatenate([cos] * reps, axis=1)
        sin_signed = jnp.concatenate([sin_signed] * reps, axis=1)
    return t * cos + swapped * sin_signed


def _gathered_rows(pos_ref, y_hbm, ybuf, ysem):
    i = pl.program_id(0)
    steps = pl.num_programs(0)
    tm = ybuf.shape[1]

    def start(step, slot):
        for r in range(tm):
            pltpu.make_async_copy(y_hbm.at[pl.ds(pos_ref[step * tm + r], 1), :], ybuf.at[slot, pl.ds(r, 1), :],
                                  ysem.at[slot]).start()

    def wait(slot):
        pltpu.make_async_copy(y_hbm.at[pl.ds(0, tm), :], ybuf.at[slot], ysem.at[slot]).wait()

    @pl.when(i == 0)
    def _():
        start(0, 0)

    slot = lax.rem(i, 2)
    wait(slot)
    start(lax.rem(i + 1, steps), 1 - slot)

    def finish():
        @pl.when(i == steps - 1)
        def _():
            wait(1 - slot)

    return ybuf[slot], finish


def _inproj_kernel(*refs, has_y):
    finish = None
    if has_y:
        pos_ref, x_ref, y_hbm, g2_ref = refs[:4]
        ybuf, ysem = refs[-2:]
        refs = refs[4:-2]
        rows, finish = _gathered_rows(pos_ref, y_hbm, ybuf, ysem)
        x = x_ref[...] + g2_ref[0] * rows
    else:
        x_ref = refs[0]
        refs = refs[1:]
        x = x_ref[...]
    g_ref, sc_ref, sh_ref, w_ref, cos_ref, sin_ref = refs[:6]
    outs = refs[6:]
    if has_y:
        outs[0][...] = x
        outs = outs[1:]
    qa_ref, ka_ref, va_ref, ra_ref, qb_ref, kb_ref, vb_ref, gta_ref, gtb_ref, lr_ref = outs
    h = _norm_mod(x, g_ref[...], sc_ref[0], sh_ref[0]).astype(BF16)

    def mm(name):
        c0, c1 = _W_OFF[name]
        return jnp.dot(h, w_ref[:, c0:c1], preferred_element_type=F32)

    qa_ref[...] = (mm("qa") * GLA_HK ** -0.5).astype(BF16)
    ka_ref[...] = mm("ka").astype(BF16)
    va_ref[...] = mm("va").astype(BF16)
    ra_ref[...] = mm("ra").astype(BF16)
    cos = cos_ref[...]
    sin = sin_ref[...]
    qb_ref[...] = (_rope(mm("qb"), cos, sin) * HEAD_DIM ** -0.5).astype(BF16)
    kb_ref[...] = _rope(mm("kb"), cos, sin).astype(BF16)
    vb_ref[...] = mm("vb").astype(BF16)
    gta_ref[...] = mm("gta").astype(BF16)
    gtb_ref[...] = mm("gtb").astype(BF16)
    lr_ref[...] = mm("lr")
    if finish is not None:
        finish()


def _inproj(x2, y, pos, g2, g, sc, sh, w, cos, sin, seq):
    n, d = x2.shape
    tm = TM_PROJ
    per_b = seq // tm
    row = lambda i, *_: (i, 0)
    const = lambda i, *_: (0, 0)
    bsel = lambda i, *_: (i // per_b, 0, 0)
    has_y = y is not None
    widths = [(wd, BF16) for _, wd in _W_SEGS[:-1]] + [(LANES, F32)]
    if has_y:
        widths = [(d, F32)] + widths
    resid_specs = [pl.BlockSpec(memory_space=pl.ANY), pl.BlockSpec((1, 1, d), bsel)] if has_y else []
    args = (pos, x2, y, g2) if has_y else (x2,)
    scratch = [pltpu.VMEM((2, tm, d), F32), pltpu.SemaphoreType.DMA((2,))] if has_y else []
    return pl.pallas_call(
        functools.partial(_inproj_kernel, has_y=has_y),
        out_shape=tuple(jax.ShapeDtypeStruct((n, wd), dt) for wd, dt in widths),
        grid_spec=pltpu.PrefetchScalarGridSpec(
            num_scalar_prefetch=1 if has_y else 0,
            grid=(n // tm,),
            in_specs=[pl.BlockSpec((tm, d), row)] + resid_specs + [
                      pl.BlockSpec((1, d), const),
                      pl.BlockSpec((1, 1, d), bsel),
                      pl.BlockSpec((1, 1, d), bsel),
                      pl.BlockSpec((d, D_IN_PAD), const, pipeline_mode=pl.Buffered(1)),
                      pl.BlockSpec((tm, LANES), row),
                      pl.BlockSpec((tm, LANES), row)],
            out_specs=tuple(pl.BlockSpec((tm, wd), row) for wd, _ in widths),
            scratch_shapes=scratch),
        compiler_params=_params(("arbitrary",)),
        name="norm_inproj",
    )(*args, g, sc, sh, w, cos, sin)


def _gla_kernel(q_ref, k_ref, v_ref, r_ref, lr_ref, aw_ref, ab_ref, gn_ref, tri_ref, o_ref, b_scr, st_scr):
    tg = q_ref.shape[1]
    c = GLA_CHUNK

    @pl.when(pl.program_id(1) == 0)
    def _():
        st_scr[...] = jnp.zeros_like(st_scr)

    pre = jnp.dot(lr_ref[0].astype(BF16), aw_ref[...], preferred_element_type=F32) + ab_ref[...]
    g = (jnp.minimum(pre, 0.0) - jnp.log1p(jnp.exp(-jnp.abs(pre)))) * (1.0 / GLA_GATE_NORM)
    ghi, glo = _split_bf16(g)
    tri = tri_ref[...]
    b_scr[...] = jnp.dot(tri, ghi, preferred_element_type=F32) + jnp.dot(tri, glo, preferred_element_type=F32)

    row = lax.broadcasted_iota(jnp.int32, (c, c), 0)
    col = lax.broadcasted_iota(jnp.int32, (c, c), 1)
    causal = col <= row
    gn = gn_ref[...]

    def chunk(ci, carry):
        r0 = pl.multiple_of(ci * c, c)
        rows = pl.ds(r0, c)
        for hh in range(GLA_HEADS):
            kc = slice(hh * GLA_HK, (hh + 1) * GLA_HK)
            vc = slice(hh * GLA_HV, (hh + 1) * GLA_HV)
            b = b_scr[rows, kc]
            b_last = b[c - 1:c, :]
            q = q_ref[0, rows, kc].astype(F32)
            k = k_ref[0, rows, kc].astype(F32)
            v = v_ref[0, rows, vc]
            q_e = (q * jnp.exp(b)).astype(BF16)
            k_i = (k * jnp.exp(-b)).astype(BF16)
            k_s = (k * jnp.exp(b_last - b)).astype(BF16)
            att = lax.dot_general(q_e, k_i, (((1,), (1,)), ((), ())), preferred_element_type=F32)
            att = jnp.where(causal, att, 0.0).astype(BF16)
            st = st_scr[hh]
            o = jnp.dot(att, v, preferred_element_type=F32)
            o += lax.dot_general(q_e, st.astype(BF16), (((1,), (1,)), ((), ())), preferred_element_type=F32)
            upd = lax.dot_general(v, k_s, (((0,), (0,)), ((), ())), preferred_element_type=F32)
            st_scr[hh] = st * jnp.exp(b_last) + upd
            ms = jnp.mean(o * o, axis=-1, keepdims=True)
            y = o * lax.rsqrt(ms + EPS) * gn
            r = r_ref[0, rows, vc].astype(F32)
            o_ref[0, rows, vc] = (y * _silu(r)).astype(BF16)
        return carry

    lax.fori_loop(0, tg // c, chunk, 0, unroll=True)


def _gla(qa, ka, va, ra, lr, aw, ab, gn, batch, seq):
    tg = TG_GLA
    blk = lambda b, s: (b, s, 0)
    const = lambda b, s: (0, 0)
    t = np.arange(tg)
    tri = jnp.asarray(((t[:, None] // GLA_CHUNK == t[None, :] // GLA_CHUNK) & (t[None, :] <= t[:, None])), BF16)
    r3 = lambda a: a.reshape(batch, seq, a.shape[-1])
    out = pl.pallas_call(
        _gla_kernel,
        out_shape=jax.ShapeDtypeStruct((batch, seq, GLA_DV), BF16),
        grid=(batch, seq // tg),
        in_specs=[pl.BlockSpec((1, tg, GLA_DK), blk),
                  pl.BlockSpec((1, tg, GLA_DK), blk),
                  pl.BlockSpec((1, tg, GLA_DV), blk),
                  pl.BlockSpec((1, tg, GLA_DV), blk),
                  pl.BlockSpec((1, tg, LANES), blk),
                  pl.BlockSpec((LANES, GLA_DK), const),
                  pl.BlockSpec((1, GLA_DK), const),
                  pl.BlockSpec((1, GLA_HV), const),
                  pl.BlockSpec((tg, tg), const)],
        out_specs=pl.BlockSpec((1, tg, GLA_DV), blk),
        scratch_shapes=[pltpu.VMEM((tg, GLA_DK), F32), pltpu.VMEM((GLA_HEADS, GLA_HV, GLA_HK), F32)],
        compiler_params=_params(("arbitrary", "arbitrary")),
        name="gla",
    )(r3(qa), r3(ka), r3(va), r3(ra), r3(lr), aw, ab, gn, tri)
    return out.reshape(batch * seq, GLA_DV)


def _swa_kernel(sink_ref, q_ref, kp_ref, kc_ref, vp_ref, vc_ref, o_ref, s_scr, p_scr):
    w = WINDOW
    blk = pl.program_id(1)
    t = lax.broadcasted_iota(jnp.int32, (w, 2 * w), 0)
    j = lax.broadcasted_iota(jnp.int32, (w, 2 * w), 1)
    diff = w + t - j
    valid = (diff >= 0) & (diff < w) & (blk * w + j - w >= 0)
    bias = jnp.where(valid, 0.0, NEG_BIG)

    lane = lax.broadcasted_iota(jnp.int32, (2 * w, LANES), 1)
    low = lane < HEAD_DIM

    def halves(prev_ref, cur_ref):
        both = jnp.concatenate([prev_ref[0], cur_ref[0]], axis=0)
        swapped = pltpu.roll(both, HEAD_DIM, 1)
        zero = jnp.zeros_like(both)
        kv0 = (jnp.where(low, both, zero), jnp.where(low, zero, swapped))
        kv1 = (jnp.where(low, swapped, zero), jnp.where(low, zero, both))
        return kv0, kv1

    k_halves = halves(kp_ref, kc_ref)
    v_halves = halves(vp_ref, vc_ref)
    pairs = SWA_GROUP // 2
    nt = (((1,), (1,)), ((), ()))
    for c in range(SWA_KV_HEADS):
        q4 = jnp.concatenate([q_ref[0, :, (c * pairs + p) * LANES:(c * pairs + p + 1) * LANES]
                              for p in range(pairs)], axis=0)
        for par in range(2):
            s_scr[par] = lax.dot_general(q4, k_halves[c][par], nt, preferred_element_type=F32)
        for par in range(2):
            for p in range(pairs):
                sink = sink_ref[c * SWA_GROUP + 2 * p + par]
                rows = slice(p * w, (p + 1) * w)
                s = s_scr[par, rows, :] + bias
                m = jnp.maximum(jnp.max(s, axis=-1, keepdims=True), sink)
                e = jnp.exp(s - m)
                denom = jnp.sum(e, axis=-1, keepdims=True) + jnp.exp(sink - m)
                p_scr[par, rows, :] = (e * (1.0 / denom)).astype(BF16)
        o4 = (jnp.dot(p_scr[0], v_halves[c][0], preferred_element_type=F32)
              + jnp.dot(p_scr[1], v_halves[c][1], preferred_element_type=F32))
        for p in range(pairs):
            o_ref[0, :, (c * pairs + p) * LANES:(c * pairs + p + 1) * LANES] = o4[p * w:(p + 1) * w].astype(BF16)


def _swa(qb, kb, vb, sinks, batch, seq):
    w = WINDOW
    r3 = lambda a: a.reshape(batch, seq, a.shape[-1])
    cur = lambda b, s, sk: (b, s, 0)
    prev = lambda b, s, sk: (b, jnp.maximum(s - 1, 0), 0)
    out = pl.pallas_call(
        _swa_kernel,
        out_shape=jax.ShapeDtypeStruct((batch, seq, SWA_DQ), BF16),
        grid_spec=pltpu.PrefetchScalarGridSpec(
            num_scalar_prefetch=1,
            grid=(batch, seq // w),
            in_specs=[pl.BlockSpec((1, w, SWA_DQ), cur),
                      pl.BlockSpec((1, w, SWA_DKV), prev),
                      pl.BlockSpec((1, w, SWA_DKV), cur),
                      pl.BlockSpec((1, w, SWA_DKV), prev),
                      pl.BlockSpec((1, w, SWA_DKV), cur)],
            out_specs=pl.BlockSpec((1, w, SWA_DQ), cur),
            scratch_shapes=[pltpu.VMEM((2, SWA_GROUP // 2 * w, 2 * w), F32),
                            pltpu.VMEM((2, SWA_GROUP // 2 * w, 2 * w), BF16)]),
        compiler_params=_params(("arbitrary", "arbitrary")),
        name="swa",
    )(sinks, r3(qb), r3(kb), r3(kb), r3(vb), r3(vb))
    return out.reshape(batch * seq, SWA_DQ)


def _top2_sum(a):
    best = a[0] + a[1]
    for i in range(len(a)):
        for j in range(i + 1, len(a)):
            if (i, j) != (0, 1):
                best = jnp.maximum(best, a[i] + a[j])
    return best


_PAIR_SLOTS = ((0, 1), (2, 1), (2, 0), (3, 0), (3, 1), (3, 2))
N_CLASSES = N_GROUPS * len(_PAIR_SLOTS)
CLASS_ROWS = 32
_CLASS_EA = tuple(g * EXPERTS_PER_GROUP + a for g in range(N_GROUPS) for a, _ in _PAIR_SLOTS)
_CLASS_EB = tuple(g * EXPERTS_PER_GROUP + b for g in range(N_GROUPS) for _, b in _PAIR_SLOTS)


def _route(logits_t, rb):
    scores = _sigmoid(logits_t)
    biased = scores + rb
    sc = [scores[e:e + 1, :] for e in range(N_EXPERTS)]
    bi = [biased[e:e + 1, :] for e in range(N_EXPERTS)]
    gs = [_top2_sum(bi[g * EXPERTS_PER_GROUP:(g + 1) * EXPERTS_PER_GROUP]) for g in range(N_GROUPS)]
    best = gs[0]
    for g in range(1, N_GROUPS):
        best = jnp.maximum(best, gs[g])
    taken = jnp.zeros_like(best, dtype=jnp.bool_)
    in_grp = []
    for g in range(N_GROUPS):
        sel = (gs[g] == best) & jnp.logical_not(taken)
        taken = taken | sel
        in_grp.append(sel)
    cand = [jnp.where(in_grp[e // EXPERTS_PER_GROUP], bi[e], -jnp.inf) for e in range(N_EXPERTS)]
    picked = [None] * N_EXPERTS
    chosen = [jnp.zeros_like(taken) for _ in range(N_EXPERTS)]
    for _ in range(2):
        top = cand[0]
        for e in range(1, N_EXPERTS):
            top = jnp.maximum(top, cand[e])
        done = jnp.zeros_like(taken)
        for e in range(N_EXPERTS):
            hit = (cand[e] == top) & jnp.logical_not(done)
            done = done | hit
            chosen[e] = chosen[e] | hit
            cand[e] = jnp.where(hit, -jnp.inf, cand[e])
    total = jnp.zeros_like(best)
    for e in range(N_EXPERTS):
        picked[e] = jnp.where(chosen[e], sc[e], 0.0)
        total = total + picked[e]
    ind = [chosen[_CLASS_EA[k]] & chosen[_CLASS_EB[k]] for k in range(N_CLASSES)]
    w_a = jnp.zeros_like(best)
    w_b = jnp.zeros_like(best)
    for k in range(N_CLASSES):
        w_a = w_a + jnp.where(ind[k], picked[_CLASS_EA[k]], 0.0)
        w_b = w_b + jnp.where(ind[k], picked[_CLASS_EB[k]], 0.0)
    ind_rows = [jnp.where(i, 1.0, 0.0) for i in ind] + [jnp.zeros_like(best)] * (CLASS_ROWS - N_CLASSES)
    return jnp.concatenate(ind_rows, axis=0), w_a / total, w_b / total


def _merge_kernel(x_ref, ya_ref, yb_ref, gta_ref, gtb_ref, wpa_ref, wpb_ref, wo_ref, g1_ref, n2_ref, sc2_ref, sh2_ref,
                  rwt_ref, rb_ref, tri_ref, xo_ref, hx_ref, info_ref, cnt_ref, carry_scr):
    @pl.when(pl.program_id(0) == 0)
    def _():
        carry_scr[...] = jnp.zeros_like(carry_scr)

    pa = jnp.dot(ya_ref[...], wpa_ref[...], preferred_element_type=F32)
    pb = jnp.dot(yb_ref[...], wpb_ref[...], preferred_element_type=F32)
    merged = _sigmoid(gta_ref[...].astype(F32)) * pa + _sigmoid(gtb_ref[...].astype(F32)) * pb
    mix = jnp.dot(merged.astype(BF16), wo_ref[...], preferred_element_type=F32)
    x = x_ref[...] + g1_ref[0] * mix
    xo_ref[...] = x
    h2 = _norm_mod(x, n2_ref[...], sc2_ref[0], sh2_ref[0])
    d = h2.shape[1]
    tm = h2.shape[0]
    hhi, hlo = _split_bf16(h2)
    whi, wlo = _split_bf16(rwt_ref[...])
    nt = (((1,), (1,)), ((), ()))
    logits_t = (lax.dot_general(whi, hhi, nt, preferred_element_type=F32)
                + lax.dot_general(whi, hlo, nt, preferred_element_type=F32)
                + lax.dot_general(wlo, hhi, nt, preferred_element_type=F32))
    ind, w_a, w_b = _route(logits_t, rb_ref[...])
    hx_ref[:, :d] = h2
    hx_ref[:, d:] = jnp.concatenate([w_a, w_b, jnp.zeros((LANES - 2, tm), F32)], axis=0).T
    carry = carry_scr[:, 0:1]
    cum = jnp.dot(ind.astype(BF16), tri_ref[...], preferred_element_type=F32)
    k_col = lax.broadcasted_iota(jnp.int32, (CLASS_ROWS, 1), 0).astype(F32)
    cls = jnp.sum(ind * k_col, axis=0, keepdims=True)
    rank = jnp.sum(ind * (cum + carry), axis=0, keepdims=True) - 1.0
    info = jnp.concatenate([cls, rank, jnp.zeros((6, tm), F32)], axis=0)
    info_ref[0] = info.astype(jnp.int32)
    carry = carry + jnp.sum(ind, axis=1, keepdims=True)
    carry_scr[...] = jnp.broadcast_to(carry, carry_scr.shape)
    cnt_ref[...] = jnp.broadcast_to(carry, cnt_ref.shape).astype(jnp.int32)


def _merge(x2, ya, yb, gta, gtb, wpa, wpb, wo, g1, n2, sc2, sh2, rwt, rb, seq):
    n, d = x2.shape
    tm = TM_PROJ
    per_b = seq // tm
    row = lambda i: (i, 0)
    const = lambda i: (0, 0)
    bsel = lambda i: (i // per_b, 0, 0)
    wspec = pl.BlockSpec((d, d), const, pipeline_mode=pl.Buffered(1))
    t = np.arange(tm)
    tri = jnp.asarray(t[:, None] <= t[None, :], BF16)
    return pl.pallas_call(
        _merge_kernel,
        out_shape=(jax.ShapeDtypeStruct((n, d), F32), jax.ShapeDtypeStruct((n, d + LANES), F32),
                   jax.ShapeDtypeStruct((n // tm, 8, tm), jnp.int32),
                   jax.ShapeDtypeStruct((CLASS_ROWS, LANES), jnp.int32)),
        grid=(n // tm,),
        in_specs=[pl.BlockSpec((tm, d), row)] * 5 + [wspec, wspec, wspec,
                  pl.BlockSpec((1, 1, d), bsel),
                  pl.BlockSpec((1, d), const),
                  pl.BlockSpec((1, 1, d), bsel),
                  pl.BlockSpec((1, 1, d), bsel),
                  pl.BlockSpec((N_EXPERTS, d), const),
                  pl.BlockSpec((N_EXPERTS, 1), const),
                  pl.BlockSpec((tm, tm), const)],
        out_specs=(pl.BlockSpec((tm, d), row), pl.BlockSpec((tm, d + LANES), row),
                   pl.BlockSpec((1, 8, tm), lambda i: (i, 0, 0)),
                   pl.BlockSpec((CLASS_ROWS, LANES), const)),
        scratch_shapes=[pltpu.VMEM((CLASS_ROWS, LANES), F32)],
        compiler_params=_params(("arbitrary",)),
        name="merge_route",
    )(x2, ya, yb, gta, gtb, wpa, wpb, wo, g1, n2, sc2, sh2, rwt, rb, tri)


def _lookup(table, idx):
    hit = idx[:, None] == jnp.arange(table.shape[0], dtype=jnp.int32)[None, :]
    return jnp.sum(jnp.where(hit, table[None, :], 0), axis=1)


def _moe_plan(counts, n):
    tm = TM_MOE
    tiles = n // tm + N_CLASSES
    padded = (counts + tm - 1) // tm * tm
    ends = jnp.cumsum(padded)
    bounds = jnp.concatenate([jnp.zeros((1,), jnp.int32), ends]).astype(jnp.int32)
    nvt = ends[-1] // tm
    pos = jnp.minimum(jnp.arange(tiles, dtype=jnp.int32), nvt - 1) * tm
    k = jnp.sum((ends[None, :] <= pos[:, None]).astype(jnp.int32), axis=1)
    ea = _lookup(jnp.asarray(_CLASS_EA, jnp.int32), k)
    eb = _lookup(jnp.asarray(_CLASS_EB, jnp.int32), k)
    first = jnp.ones((1,), jnp.int32)
    idx = jnp.arange(tiles, dtype=jnp.int32)

    def slot_plan(e):
        ch = jnp.concatenate([first, (e[1:] != e[:-1]).astype(jnp.int32)])
        at = jnp.where(ch == 1, idx, tiles)
        nxt = jnp.min(jnp.where(idx[None, :] > idx[:, None], at[None, :], tiles), axis=1)
        has = (nxt < tiles).astype(jnp.int32)
        return ch, _lookup(e, jnp.minimum(nxt, tiles - 1)), has

    cha, nxa, hasa = slot_plan(ea)
    chb, nxb, hasb = slot_plan(eb)
    return ea, eb, cha, chb, nxa, nxb, hasa, hasb, bounds, nvt.reshape(1).astype(jnp.int32)


def _moe_kernel(ea_ref, eb_ref, cha_ref, chb_ref, nxa_ref, nxb_ref, hasa_ref, hasb_ref, bnd_ref, cnt_ref, nvt_ref,
                pos_ref, hx_hbm, wg_hbm, wu_hbm, wd_hbm, y_ref,
                buf, wg_st, wu_st, wd_st, wg_bf, wu_bf, wd_bf, perm, gsem, wsem, *, n_tokens, layer):
    i = pl.program_id(0)
    tm = TM_MOE
    d = y_ref.shape[1]
    nvt = nvt_ref[0]

    def weight_copies(slot, e):
        return [pltpu.make_async_copy(src.at[layer, e], dst.at[slot], wsem.at[slot])
                for src, dst in ((wg_hbm, wg_st), (wu_hbm, wu_st), (wd_hbm, wd_st))]

    def refresh(slot, has_next, next_e):
        for cp in weight_copies(slot, 0):
            cp.wait()
        wg_bf[slot] = wg_st[slot].astype(BF16)
        wu_bf[slot] = wu_st[slot].astype(BF16)
        wd_bf[slot] = wd_st[slot].astype(BF16)

        @pl.when(has_next == 1)
        def _():
            for cp in weight_copies(slot, next_e):
                cp.start()

    def gather_start(tile, slot):
        for r in range(tm):
            pltpu.make_async_copy(hx_hbm.at[pl.ds(perm[tile * tm + r], 1), :], buf.at[slot, pl.ds(r, 1), :],
                                  gsem.at[slot]).start()

    def gather_wait(slot):
        pltpu.make_async_copy(hx_hbm.at[pl.ds(0, tm), :], buf.at[slot], gsem.at[slot]).wait()

    @pl.when(i == 0)
    def _():
        for cp in weight_copies(0, ea_ref[0]) + weight_copies(1, eb_ref[0]):
            cp.start()

        def place(t, carry):
            perm[pos_ref[t]] = t
            return carry

        lax.fori_loop(0, n_tokens, place, 0, unroll=8)

        def pad(p, carry):
            perm[p] = n_tokens - 1
            return carry

        for k in range(N_CLASSES):
            lax.fori_loop(bnd_ref[k] + cnt_ref[k], bnd_ref[k + 1], pad, 0)
        lax.fori_loop(bnd_ref[N_CLASSES], bnd_ref[N_CLASSES] + tm, pad, 0)
        gather_start(0, 0)

    @pl.when((i < nvt) & (cha_ref[i] == 1))
    def _():
        refresh(0, hasa_ref[i], nxa_ref[i])

    @pl.when((i < nvt) & (chb_ref[i] == 1))
    def _():
        refresh(1, hasb_ref[i], nxb_ref[i])

    @pl.when(i < nvt)
    def _():
        slot = lax.rem(i, 2)
        gather_wait(slot)
        gather_start(i + 1, 1 - slot)
        hx = buf[slot]
        h = hx[:, :d].astype(BF16)
        acc = None
        for s in range(2):
            w = hx[:, d + s:d + s + 1]
            gate = jnp.dot(h, wg_bf[s], preferred_element_type=F32)
            up = jnp.dot(h, wu_bf[s], preferred_element_type=F32)
            hid = (_silu(gate) * up * w).astype(BF16)
            part = jnp.dot(hid, wd_bf[s], preferred_element_type=F32)
            acc = part if acc is None else acc + part
        y_ref[...] = acc

    @pl.when(i >= nvt)
    def _():
        y_ref[...] = jnp.zeros_like(y_ref)

    @pl.when(i == nvt)
    def _():
        gather_wait(lax.rem(nvt, 2))


def _moe(hx, info, cnt, w_gate, w_up, w_down, layer):
    n, dx = hx.shape
    d = dx - LANES
    tm = TM_MOE
    tiles = n // tm + N_CLASSES
    counts = cnt[:, 0]
    ea, eb, cha, chb, nxa, nxb, hasa, hasb, bounds, nvt = _moe_plan(counts, n)
    pos = _lookup(bounds, info[:, 0, :].reshape(n)) + info[:, 1, :].reshape(n)
    f = w_gate.shape[-1]
    hbm = pl.BlockSpec(memory_space=pl.ANY)
    y = pl.pallas_call(
        functools.partial(_moe_kernel, n_tokens=n, layer=layer),
        out_shape=jax.ShapeDtypeStruct((tiles * tm, d), F32),
        grid_spec=pltpu.PrefetchScalarGridSpec(
            num_scalar_prefetch=12,
            grid=(tiles,),
            in_specs=[hbm, hbm, hbm, hbm],
            out_specs=pl.BlockSpec((tm, d), lambda i, *_: (i, 0)),
            scratch_shapes=[pltpu.VMEM((2, tm, dx), F32),
                            pltpu.VMEM((2, d, f), F32), pltpu.VMEM((2, d, f), F32), pltpu.VMEM((2, f, d), F32),
                            pltpu.VMEM((2, d, f), BF16), pltpu.VMEM((2, d, f), BF16), pltpu.VMEM((2, f, d), BF16),
                            pltpu.SMEM((tiles * tm,), jnp.int32),
                            pltpu.SemaphoreType.DMA((2,)), pltpu.SemaphoreType.DMA((2,))]),
        compiler_params=_params(("arbitrary",)),
        name="moe",
    )(ea, eb, cha, chb, nxa, nxb, hasa, hasb, bounds, counts, nvt, pos, hx, w_gate, w_up, w_down)
    return y, pos


def _final_kernel(pos_ref, x_ref, y_hbm, g2_ref, fin_ref, o_ref, ybuf, ysem):
    rows, finish = _gathered_rows(pos_ref, y_hbm, ybuf, ysem)
    x = x_ref[...] + g2_ref[0] * rows
    ms = jnp.mean(x * x, axis=-1, keepdims=True)
    o_ref[...] = x * lax.rsqrt(ms + EPS) * fin_ref[...]
    finish()


def _final(x2, y, pos, g2, fin, seq):
    n, d = x2.shape
    tm = TM_PROJ
    per_b = seq // tm
    row = lambda i, *_: (i, 0)
    return pl.pallas_call(
        _final_kernel,
        out_shape=jax.ShapeDtypeStruct((n, d), F32),
        grid_spec=pltpu.PrefetchScalarGridSpec(
            num_scalar_prefetch=1,
            grid=(n // tm,),
            in_specs=[pl.BlockSpec((tm, d), row), pl.BlockSpec(memory_space=pl.ANY),
                      pl.BlockSpec((1, 1, d), lambda i, *_: (i // per_b, 0, 0)),
                      pl.BlockSpec((1, d), lambda i, *_: (0, 0))],
            out_specs=pl.BlockSpec((tm, d), row),
            scratch_shapes=[pltpu.VMEM((2, tm, d), F32), pltpu.SemaphoreType.DMA((2,))]),
        compiler_params=_params(("arbitrary",)),
        name="final_norm",
    )(pos, x2, y, g2, fin)


def _pack_w_in(w_in_l):
    parts = jnp.split(w_in_l, [int(i) for i in np.cumsum(
        (GLA_DK, GLA_DK, GLA_DV, GLA_DV, GLA_RANK, SWA_DQ, SWA_DKV, SWA_DKV, D_MODEL))], axis=-1)
    q_a, k_a, v_a, r_a, lr_a, q_b, k_b, v_b, gt_a, gt_b = parts
    lr_pad = jnp.pad(lr_a, ((0, 0), (0, LANES - GLA_RANK)))
    return jnp.concatenate([q_a, k_a, v_a, r_a, q_b, k_b, v_b, gt_a, gt_b, lr_pad], axis=-1).astype(BF16)


def kernel(x, c, positions, ada_w, ada_b, norm1_g, norm2_g, final_g, w_in, gla_alpha_w, gla_alpha_b, gla_norm_g,
           swa_sinks, w_pa, w_pb, w_out, router_w, router_b, moe_w_gate, moe_w_up, moe_w_down):
    batch, seq, d = x.shape
    n = batch * seq
    cos, sin = _rope_tables(positions)
    mod = _modulation(c, ada_w, ada_b)
    rwt = router_w.T
    rb = router_b.reshape(N_EXPERTS, 1)
    x2 = x.reshape(n, d)
    y = pos = g2 = None
    for l in range(DEPTH):
        m = mod[l].reshape(batch, N_MOD, 1, d)
        sh1, sc1, g1, sh2, sc2, g2_l = (m[:, i] for i in range(N_MOD))
        outs = _inproj(x2, y, pos, g2, norm1_g[l][None, :], sc1, sh1, _pack_w_in(w_in[l]), cos, sin, seq)
        if y is not None:
            x2, outs = outs[0], outs[1:]
        qa, ka, va, ra, qb, kb, vb, gta, gtb, lr = outs
        aw = jnp.pad(gla_alpha_w[l], ((0, LANES - GLA_RANK), (0, 0))).astype(BF16)
        ya = _gla(qa, ka, va, ra, lr, aw, gla_alpha_b[l][None, :], gla_norm_g[l][None, :], batch, seq)
        yb = _swa(qb, kb, vb, swa_sinks[l], batch, seq)
        x2, hx, info, cnt = _merge(x2, ya, yb, gta, gtb, w_pa[l].astype(BF16), w_pb[l].astype(BF16),
                                   w_out[l].astype(BF16), g1, norm2_g[l][None, :], sc2, sh2, rwt, rb, seq)
        y, pos = _moe(hx, info, cnt, moe_w_gate, moe_w_up, moe_w_down, l)
        g2 = g2_l
    return _final(x2, y, pos, g2, final_g[None, :], seq).reshape(batch, seq, d)
```

```python
import functools

import jax
import jax.numpy as jnp
import numpy as np
from jax import lax
from jax.experimental import pallas as pl
from jax.experimental.pallas import tpu as pltpu

F32 = jnp.float32
BF16 = jnp.bfloat16

D_MODEL = 1024
DEPTH = 4
GLA_HEADS = 4
GLA_DK = D_MODEL // 2
GLA_DV = D_MODEL
GLA_HK = GLA_DK // GLA_HEADS
GLA_HV = GLA_DV // GLA_HEADS
GLA_RANK = 16
GLA_GATE_NORM = 16.0
GLA_CHUNK = 64
SWA_HEADS = 16
SWA_KV_HEADS = 2
HEAD_DIM = 64
SWA_GROUP = SWA_HEADS // SWA_KV_HEADS
SWA_DQ = SWA_HEADS * HEAD_DIM
SWA_DKV = SWA_KV_HEADS * HEAD_DIM
WINDOW = 128
ROPE_THETA = 10000.0
N_EXPERTS = 16
N_GROUPS = 4
EXPERTS_PER_GROUP = N_EXPERTS // N_GROUPS
D_EXPERT = 512
N_MOD = 6
EPS = 1e-6

LANES = 128
VMEM_LIMIT = 56 * 1024 * 1024
NEG_BIG = -1e30

_W_SEGS = (("qa", GLA_DK), ("ka", GLA_DK), ("va", GLA_DV), ("ra", GLA_DV), ("qb", SWA_DQ), ("kb", SWA_DKV),
           ("vb", SWA_DKV), ("gta", D_MODEL), ("gtb", D_MODEL), ("lr", LANES))
_W_OFF = {}
_o = 0
for _n, _w in _W_SEGS:
    _W_OFF[_n] = (_o, _o + _w)
    _o += _w
D_IN_PAD = _o

TM_PROJ = 512
TG_GLA = 256
TM_MOE = 256
TS_ROPE = 2048


def _params(sem):
    return pltpu.CompilerParams(dimension_semantics=sem, vmem_limit_bytes=VMEM_LIMIT)


def _split_bf16(a):
    hi = a.astype(BF16)
    lo = (a - hi.astype(F32)).astype(BF16)
    return hi, lo


def _sigmoid(x):
    return 1.0 / (1.0 + jnp.exp(-x))


def _silu(x):
    return x * _sigmoid(x)


def _rope_table_kernel(pos_ref, invf_ref, sign_ref, cos_ref, sin_ref):
    ang = pos_ref[...].astype(F32) * invf_ref[...]
    cos_ref[...] = jnp.cos(ang)
    sin_ref[...] = jnp.sin(ang) * sign_ref[...]


def _rope_tables(positions):
    n = positions.size
    half = HEAD_DIM // 2
    inv_freq = jnp.power(ROPE_THETA, -jnp.arange(half, dtype=F32) / half)
    invf = jnp.tile(inv_freq, LANES // half)[None, :]
    sign = jnp.tile(jnp.concatenate([-jnp.ones((half,), F32), jnp.ones((half,), F32)]), LANES // HEAD_DIM)[None, :]
    pos = positions.reshape(n, 1)
    return pl.pallas_call(
        _rope_table_kernel,
        out_shape=(jax.ShapeDtypeStruct((n, LANES), F32), jax.ShapeDtypeStruct((n, LANES), F32)),
        grid=(n // TS_ROPE,),
        in_specs=[pl.BlockSpec((TS_ROPE, 1), lambda i: (i, 0)),
                  pl.BlockSpec((1, LANES), lambda i: (0, 0)),
                  pl.BlockSpec((1, LANES), lambda i: (0, 0))],
        out_specs=(pl.BlockSpec((TS_ROPE, LANES), lambda i: (i, 0)),
                   pl.BlockSpec((TS_ROPE, LANES), lambda i: (i, 0))),
        compiler_params=_params(("arbitrary",)),
        name="rope_tables",
    )(pos, invf, sign)


def _mod_kernel(c_ref, w_ref, b_ref, o_ref):
    cond = _silu(c_ref[...])
    chi, clo = _split_bf16(cond)
    whi, wlo = _split_bf16(w_ref[0])
    acc = jnp.dot(chi, whi, preferred_element_type=F32)
    acc += jnp.dot(chi, wlo, preferred_element_type=F32)
    acc += jnp.dot(clo, whi, preferred_element_type=F32)
    o_ref[0] = acc + b_ref[0]


def _modulation(c, ada_w, ada_b):
    nl, d, n6 = ada_w.shape
    b = c.shape[0]
    tn = 1536
    return pl.pallas_call(
        _mod_kernel,
        out_shape=jax.ShapeDtypeStruct((nl, b, n6), F32),
        grid=(nl, n6 // tn),
        in_specs=[pl.BlockSpec((b, d), lambda l, j: (0, 0)),
                  pl.BlockSpec((1, d, tn), lambda l, j: (l, 0, j)),
                  pl.BlockSpec((1, 1, tn), lambda l, j: (l, 0, j))],
        out_specs=pl.BlockSpec((1, b, tn), lambda l, j: (l, 0, j)),
        compiler_params=_params(("arbitrary", "arbitrary")),
        name="adaln_mod",
    )(c, ada_w, ada_b.reshape(nl, 1, n6))


def _norm_mod(x, g, sc, sh):
    ms = jnp.mean(x * x, axis=-1, keepdims=True)
    return (x * lax.rsqrt(ms + EPS) * g) * (1.0 + sc) + sh


def _rope(t, cos, sin_signed):
    w = t.shape[-1]
    up = pltpu.roll(t, w - HEAD_DIM // 2, 1)
    dn = pltpu.roll(t, HEAD_DIM // 2, 1)
    lane = lax.broadcasted_iota(jnp.int32, t.shape, 1)
    swapped = jnp.where(lane % HEAD_DIM < HEAD_DIM // 2, up, dn)
    reps = w // LANES
    if reps > 1:
        cos = jnp.concatenate([cos] * reps, axis=1)
        sin_signed = jnp.concatenate([sin_signed] * reps, axis=1)
    return t * cos + swapped * sin_signed


def _gathered_rows(pos_ref, y_hbm, ybuf, ysem):
    i = pl.program_id(0)
    steps = pl.num_programs(0)
    tm = ybuf.shape[1]

    def start(step, slot):
        for r in range(tm):
            pltpu.make_async_copy(y_hbm.at[pl.ds(pos_ref[step * tm + r], 1), :], ybuf.at[slot, pl.ds(r, 1), :],
                                  ysem.at[slot]).start()

    def wait(slot):
        pltpu.make_async_copy(y_hbm.at[pl.ds(0, tm), :], ybuf.at[slot], ysem.at[slot]).wait()

    @pl.when(i == 0)
    def _():
        start(0, 0)

    slot = lax.rem(i, 2)
    wait(slot)
    start(lax.rem(i + 1, steps), 1 - slot)

    def finish():
        @pl.when(i == steps - 1)
        def _():
            wait(1 - slot)

    return ybuf[slot], finish


def _inproj_kernel(*refs, has_y):
    finish = None
    if has_y:
        pos_ref, x_ref, y_hbm, g2_ref = refs[:4]
        ybuf, ysem = refs[-2:]
        refs = refs[4:-2]
        rows, finish = _gathered_rows(pos_ref, y_hbm, ybuf, ysem)
        x = x_ref[...] + g2_ref[0] * rows
    else:
        x_ref = refs[0]
        refs = refs[1:]
        x = x_ref[...]
    g_ref, sc_ref, sh_ref, w_ref, cos_ref, sin_ref = refs[:6]
    outs = refs[6:]
    if has_y:
        outs[0][...] = x
        outs = outs[1:]
    qa_ref, ka_ref, va_ref, ra_ref, qb_ref, kb_ref, vb_ref, gta_ref, gtb_ref, lr_ref = outs
    h = _norm_mod(x, g_ref[...], sc_ref[0], sh_ref[0]).astype(BF16)

    def mm(name):
        c0, c1 = _W_OFF[name]
        return jnp.dot(h, w_ref[:, c0:c1], preferred_element_type=F32)

    qa_ref[...] = (mm("qa") * GLA_HK ** -0.5).astype(BF16)
    ka_ref[...] = mm("ka").astype(BF16)
    va_ref[...] = mm("va").astype(BF16)
    ra_ref[...] = mm("ra").astype(BF16)
    cos = cos_ref[...]
    sin = sin_ref[...]
    qb_ref[...] = (_rope(mm("qb"), cos, sin) * HEAD_DIM ** -0.5).astype(BF16)
    kb_ref[...] = _rope(mm("kb"), cos, sin).astype(BF16)
    vb_ref[...] = mm("vb").astype(BF16)
    gta_ref[...] = mm("gta").astype(BF16)
    gtb_ref[...] = mm("gtb").astype(BF16)
    lr_ref[...] = mm("lr")
    if finish is not None:
        finish()


def _inproj(x2, y, pos, g2, g, sc, sh, w, cos, sin, seq):
    n, d = x2.shape
    tm = TM_PROJ
    per_b = seq // tm
    row = lambda i, *_: (i, 0)
    const = lambda i, *_: (0, 0)
    bsel = lambda i, *_: (i // per_b, 0, 0)
    has_y = y is not None
    widths = [(wd, BF16) for _, wd in _W_SEGS[:-1]] + [(LANES, F32)]
    if has_y:
        widths = [(d, F32)] + widths
    resid_specs = [pl.BlockSpec(memory_space=pl.ANY), pl.BlockSpec((1, 1, d), bsel)] if has_y else []
    args = (pos, x2, y, g2) if has_y else (x2,)
    scratch = [pltpu.VMEM((2, tm, d), F32), pltpu.SemaphoreType.DMA((2,))] if has_y else []
    return pl.pallas_call(
        functools.partial(_inproj_kernel, has_y=has_y),
        out_shape=tuple(jax.ShapeDtypeStruct((n, wd), dt) for wd, dt in widths),
        grid_spec=pltpu.PrefetchScalarGridSpec(
            num_scalar_prefetch=1 if has_y else 0,
            grid=(n // tm,),
            in_specs=[pl.BlockSpec((tm, d), row)] + resid_specs + [
                      pl.BlockSpec((1, d), const),
                      pl.BlockSpec((1, 1, d), bsel),
                      pl.BlockSpec((1, 1, d), bsel),
                      pl.BlockSpec((d, D_IN_PAD), const, pipeline_mode=pl.Buffered(1)),
                      pl.BlockSpec((tm, LANES), row),
                      pl.BlockSpec((tm, LANES), row)],
            out_specs=tuple(pl.BlockSpec((tm, wd), row) for wd, _ in widths),
            scratch_shapes=scratch),
        compiler_params=_params(("arbitrary",)),
        name="norm_inproj",
    )(*args, g, sc, sh, w, cos, sin)


def _gla_kernel(q_ref, k_ref, v_ref, r_ref, lr_ref, aw_ref, ab_ref, gn_ref, tri_ref, o_ref, b_scr, st_scr):
    tg = q_ref.shape[1]
    c = GLA_CHUNK

    @pl.when(pl.program_id(1) == 0)
    def _():
        st_scr[...] = jnp.zeros_like(st_scr)

    pre = jnp.dot(lr_ref[0].astype(BF16), aw_ref[...], preferred_element_type=F32) + ab_ref[...]
    g = (jnp.minimum(pre, 0.0) - jnp.log1p(jnp.exp(-jnp.abs(pre)))) * (1.0 / GLA_GATE_NORM)
    ghi, glo = _split_bf16(g)
    tri = tri_ref[...]
    b_scr[...] = jnp.dot(tri, ghi, preferred_element_type=F32) + jnp.dot(tri, glo, preferred_element_type=F32)

    row = lax.broadcasted_iota(jnp.int32, (c, c), 0)
    col = lax.broadcasted_iota(jnp.int32, (c, c), 1)
    causal = col <= row
    gn = gn_ref[...]

    def chunk(ci, carry):
        r0 = pl.multiple_of(ci * c, c)
        rows = pl.ds(r0, c)
        for hh in range(GLA_HEADS):
            kc = slice(hh * GLA_HK, (hh + 1) * GLA_HK)
            vc = slice(hh * GLA_HV, (hh + 1) * GLA_HV)
            b = b_scr[rows, kc]
            b_last = b[c - 1:c, :]
            q = q_ref[0, rows, kc].astype(F32)
            k = k_ref[0, rows, kc].astype(F32)
            v = v_ref[0, rows, vc]
            q_e = (q * jnp.exp(b)).astype(BF16)
            k_i = (k * jnp.exp(-b)).astype(BF16)
            k_s = (k * jnp.exp(b_last - b)).astype(BF16)
            att = lax.dot_general(q_e, k_i, (((1,), (1,)), ((), ())), preferred_element_type=F32)
            att = jnp.where(causal, att, 0.0).astype(BF16)
            st = st_scr[hh]
            o = jnp.dot(att, v, preferred_element_type=F32)
            o += lax.dot_general(q_e, st.astype(BF16), (((1,), (1,)), ((), ())), preferred_element_type=F32)
            upd = lax.dot_general(v, k_s, (((0,), (0,)), ((), ())), preferred_element_type=F32)
            st_scr[hh] = st * jnp.exp(b_last) + upd
            ms = jnp.mean(o * o, axis=-1, keepdims=True)
            y = o * lax.rsqrt(ms + EPS) * gn
            r = r_ref[0, rows, vc].astype(F32)
            o_ref[0, rows, vc] = (y * _silu(r)).astype(BF16)
        return carry

    lax.fori_loop(0, tg // c, chunk, 0, unroll=True)


def _gla(qa, ka, va, ra, lr, aw, ab, gn, batch, seq):
    tg = TG_GLA
    blk = lambda b, s: (b, s, 0)
    const = lambda b, s: (0, 0)
    t = np.arange(tg)
    tri = jnp.asarray(((t[:, None] // GLA_CHUNK == t[None, :] // GLA_CHUNK) & (t[None, :] <= t[:, None])), BF16)
    r3 = lambda a: a.reshape(batch, seq, a.shape[-1])
    out = pl.pallas_call(
        _gla_kernel,
        out_shape=jax.ShapeDtypeStruct((batch, seq, GLA_DV), BF16),
        grid=(batch, seq // tg),
        in_specs=[pl.BlockSpec((1, tg, GLA_DK), blk),
                  pl.BlockSpec((1, tg, GLA_DK), blk),
                  pl.BlockSpec((1, tg, GLA_DV), blk),
                  pl.BlockSpec((1, tg, GLA_DV), blk),
                  pl.BlockSpec((1, tg, LANES), blk),
                  pl.BlockSpec((LANES, GLA_DK), const),
                  pl.BlockSpec((1, GLA_DK), const),
                  pl.BlockSpec((1, GLA_HV), const),
                  pl.BlockSpec((tg, tg), const)],
        out_specs=pl.BlockSpec((1, tg, GLA_DV), blk),
        scratch_shapes=[pltpu.VMEM((tg, GLA_DK), F32), pltpu.VMEM((GLA_HEADS, GLA_HV, GLA_HK), F32)],
        compiler_params=_params(("arbitrary", "arbitrary")),
        name="gla",
    )(r3(qa), r3(ka), r3(va), r3(ra), r3(lr), aw, ab, gn, tri)
    return out.reshape(batch * seq, GLA_DV)


def _swa_kernel(sink_ref, q_ref, kp_ref, kc_ref, vp_ref, vc_ref, o_ref, s_scr, p_scr):
    w = WINDOW
    blk = pl.program_id(1)
    t = lax.broadcasted_iota(jnp.int32, (w, 2 * w), 0)
    j = lax.broadcasted_iota(jnp.int32, (w, 2 * w), 1)
    diff = w + t - j
    valid = (diff >= 0) & (diff < w) & (blk * w + j - w >= 0)
    bias = jnp.where(valid, 0.0, NEG_BIG)

    lane = lax.broadcasted_iota(jnp.int32, (2 * w, LANES), 1)
    low = lane < HEAD_DIM

    def halves(prev_ref, cur_ref):
        both = jnp.concatenate([prev_ref[0], cur_ref[0]], axis=0)
        swapped = pltpu.roll(both, HEAD_DIM, 1)
        zero = jnp.zeros_like(both)
        kv0 = (jnp.where(low, both, zero), jnp.where(low, zero, swapped))
        kv1 = (jnp.where(low, swapped, zero), jnp.where(low, zero, both))
        return kv0, kv1

    k_halves = halves(kp_ref, kc_ref)
    v_halves = halves(vp_ref, vc_ref)
    pairs = SWA_GROUP // 2
    nt = (((1,), (1,)), ((), ()))
    for c in range(SWA_KV_HEADS):
        q4 = jnp.concatenate([q_ref[0, :, (c * pairs + p) * LANES:(c * pairs + p + 1) * LANES]
                              for p in range(pairs)], axis=0)
        for par in range(2):
            s_scr[par] = lax.dot_general(q4, k_halves[c][par], nt, preferred_element_type=F32)
        for par in range(2):
            for p in range(pairs):
                sink = sink_ref[c * SWA_GROUP + 2 * p + par]
                rows = slice(p * w, (p + 1) * w)
                s = s_scr[par, rows, :] + bias
                m = jnp.maximum(jnp.max(s, axis=-1, keepdims=True), sink)
                e = jnp.exp(s - m)
                denom = jnp.sum(e, axis=-1, keepdims=True) + jnp.exp(sink - m)
                p_scr[par, rows, :] = (e * (1.0 / denom)).astype(BF16)
        o4 = (jnp.dot(p_scr[0], v_halves[c][0], preferred_element_type=F32)
              + jnp.dot(p_scr[1], v_halves[c][1], preferred_element_type=F32))
        for p in range(pairs):
            o_ref[0, :, (c * pairs + p) * LANES:(c * pairs + p + 1) * LANES] = o4[p * w:(p + 1) * w].astype(BF16)


def _swa(qb, kb, vb, sinks, batch, seq):
    w = WINDOW
    r3 = lambda a: a.reshape(batch, seq, a.shape[-1])
    cur = lambda b, s, sk: (b, s, 0)
    prev = lambda b, s, sk: (b, jnp.maximum(s - 1, 0), 0)
    out = pl.pallas_call(
        _swa_kernel,
        out_shape=jax.ShapeDtypeStruct((batch, seq, SWA_DQ), BF16),
        grid_spec=pltpu.PrefetchScalarGridSpec(
            num_scalar_prefetch=1,
            grid=(batch, seq // w),
            in_specs=[pl.BlockSpec((1, w, SWA_DQ), cur),
                      pl.BlockSpec((1, w, SWA_DKV), prev),
                      pl.BlockSpec((1, w, SWA_DKV), cur),
                      pl.BlockSpec((1, w, SWA_DKV), prev),
                      pl.BlockSpec((1, w, SWA_DKV), cur)],
            out_specs=pl.BlockSpec((1, w, SWA_DQ), cur),
            scratch_shapes=[pltpu.VMEM((2, SWA_GROUP // 2 * w, 2 * w), F32),
                            pltpu.VMEM((2, SWA_GROUP // 2 * w, 2 * w), BF16)]),
        compiler_params=_params(("arbitrary", "arbitrary")),
        name="swa",
    )(sinks, r3(qb), r3(kb), r3(kb), r3(vb), r3(vb))
    return out.reshape(batch * seq, SWA_DQ)


def _top2_sum(a):
    best = a[0] + a[1]
    for i in range(len(a)):
        for j in range(i + 1, len(a)):
            if (i, j) != (0, 1):
                best = jnp.maximum(best, a[i] + a[j])
    return best


_PAIR_SLOTS = ((0, 1), (2, 1), (2, 0), (3, 0), (3, 1), (3, 2))
N_CLASSES = N_GROUPS * len(_PAIR_SLOTS)
CLASS_ROWS = 32
_CLASS_EA = tuple(g * EXPERTS_PER_GROUP + a for g in range(N_GROUPS) for a, _ in _PAIR_SLOTS)
_CLASS_EB = tuple(g * EXPERTS_PER_GROUP + b for g in range(N_GROUPS) for _, b in _PAIR_SLOTS)


def _route(logits_t, rb):
    scores = _sigmoid(logits_t)
    biased = scores + rb
    sc = [scores[e:e + 1, :] for e in range(N_EXPERTS)]
    bi = [biased[e:e + 1, :] for e in range(N_EXPERTS)]
    gs = [_top2_sum(bi[g * EXPERTS_PER_GROUP:(g + 1) * EXPERTS_PER_GROUP]) for g in range(N_GROUPS)]
    best = gs[0]
    for g in range(1, N_GROUPS):
        best = jnp.maximum(best, gs[g])
    taken = jnp.zeros_like(best, dtype=jnp.bool_)
    in_grp = []
    for g in range(N_GROUPS):
        sel = (gs[g] == best) & jnp.logical_not(taken)
        taken = taken | sel
        in_grp.append(sel)
    cand = [jnp.where(in_grp[e // EXPERTS_PER_GROUP], bi[e], -jnp.inf) for e in range(N_EXPERTS)]
    picked = [None] * N_EXPERTS
    chosen = [jnp.zeros_like(taken) for _ in range(N_EXPERTS)]
    for _ in range(2):
        top = cand[0]
        for e in range(1, N_EXPERTS):
            top = jnp.maximum(top, cand[e])
        done = jnp.zeros_like(taken)
        for e in range(N_EXPERTS):
            hit = (cand[e] == top) & jnp.logical_not(done)
            done = done | hit
            chosen[e] = chosen[e] | hit
            cand[e] = jnp.where(hit, -jnp.inf, cand[e])
    total = jnp.zeros_like(best)
    for e in range(N_EXPERTS):
        picked[e] = jnp.where(chosen[e], sc[e], 0.0)
        total = total + picked[e]
    ind = [chosen[_CLASS_EA[k]] & chosen[_CLASS_EB[k]] for k in range(N_CLASSES)]
    w_a = jnp.zeros_like(best)
    w_b = jnp.zeros_like(best)
    for k in range(N_CLASSES):
        w_a = w_a + jnp.where(ind[k], picked[_CLASS_EA[k]], 0.0)
        w_b = w_b + jnp.where(ind[k], picked[_CLASS_EB[k]], 0.0)
    ind_rows = [jnp.where(i, 1.0, 0.0) for i in ind] + [jnp.zeros_like(best)] * (CLASS_ROWS - N_CLASSES)
    return jnp.concatenate(ind_rows, axis=0), w_a / total, w_b / total


def _merge_kernel(x_ref, ya_ref, yb_ref, gta_ref, gtb_ref, wpa_ref, wpb_ref, wo_ref, g1_ref, n2_ref, sc2_ref, sh2_ref,
                  rwt_ref, rb_ref, tri_ref, xo_ref, hx_ref, info_ref, cnt_ref, carry_scr):
    @pl.when(pl.program_id(0) == 0)
    def _():
        carry_scr[...] = jnp.zeros_like(carry_scr)

    pa = jnp.dot(ya_ref[...], wpa_ref[...], preferred_element_type=F32)
    pb = jnp.dot(yb_ref[...], wpb_ref[...], preferred_element_type=F32)
    merged = _sigmoid(gta_ref[...].astype(F32)) * pa + _sigmoid(gtb_ref[...].astype(F32)) * pb
    mix = jnp.dot(merged.astype(BF16), wo_ref[...], preferred_element_type=F32)
    x = x_ref[...] + g1_ref[0] * mix
    xo_ref[...] = x
    h2 = _norm_mod(x, n2_ref[...], sc2_ref[0], sh2_ref[0])
    d = h2.shape[1]
    tm = h2.shape[0]
    hhi, hlo = _split_bf16(h2)
    whi, wlo = _split_bf16(rwt_ref[...])
    nt = (((1,), (1,)), ((), ()))
    logits_t = (lax.dot_general(whi, hhi, nt, preferred_element_type=F32)
                + lax.dot_general(whi, hlo, nt, preferred_element_type=F32)
                + lax.dot_general(wlo, hhi, nt, preferred_element_type=F32))
    ind, w_a, w_b = _route(logits_t, rb_ref[...])
    hx_ref[:, :d] = h2
    hx_ref[:, d:] = jnp.concatenate([w_a, w_b, jnp.zeros((LANES - 2, tm), F32)], axis=0).T
    carry = carry_scr[:, 0:1]
    cum = jnp.dot(ind.astype(BF16), tri_ref[...], preferred_element_type=F32)
    k_col = lax.broadcasted_iota(jnp.int32, (CLASS_ROWS, 1), 0).astype(F32)
    cls = jnp.sum(ind * k_col, axis=0, keepdims=True)
    rank = jnp.sum(ind * (cum + carry), axis=0, keepdims=True) - 1.0
    info = jnp.concatenate([cls, rank, jnp.zeros((6, tm), F32)], axis=0)
    info_ref[0] = info.astype(jnp.int32)
    carry = carry + jnp.sum(ind, axis=1, keepdims=True)
    carry_scr[...] = jnp.broadcast_to(carry, carry_scr.shape)
    cnt_ref[...] = jnp.broadcast_to(carry, cnt_ref.shape).astype(jnp.int32)


def _merge(x2, ya, yb, gta, gtb, wpa, wpb, wo, g1, n2, sc2, sh2, rwt, rb, seq):
    n, d = x2.shape
    tm = TM_PROJ
    per_b = seq // tm
    row = lambda i: (i, 0)
    const = lambda i: (0, 0)
    bsel = lambda i: (i // per_b, 0, 0)
    wspec = pl.BlockSpec((d, d), const, pipeline_mode=pl.Buffered(1))
    t = np.arange(tm)
    tri = jnp.asarray(t[:, None] <= t[None, :], BF16)
    return pl.pallas_call(
        _merge_kernel,
        out_shape=(jax.ShapeDtypeStruct((n, d), F32), jax.ShapeDtypeStruct((n, d + LANES), F32),
                   jax.ShapeDtypeStruct((n // tm, 8, tm), jnp.int32),
                   jax.ShapeDtypeStruct((CLASS_ROWS, LANES), jnp.int32)),
        grid=(n // tm,),
        in_specs=[pl.BlockSpec((tm, d), row)] * 5 + [wspec, wspec, wspec,
                  pl.BlockSpec((1, 1, d), bsel),
                  pl.BlockSpec((1, d), const),
                  pl.BlockSpec((1, 1, d), bsel),
                  pl.BlockSpec((1, 1, d), bsel),
                  pl.BlockSpec((N_EXPERTS, d), const),
                  pl.BlockSpec((N_EXPERTS, 1), const),
                  pl.BlockSpec((tm, tm), const)],
        out_specs=(pl.BlockSpec((tm, d), row), pl.BlockSpec((tm, d + LANES), row),
                   pl.BlockSpec((1, 8, tm), lambda i: (i, 0, 0)),
                   pl.BlockSpec((CLASS_ROWS, LANES), const)),
        scratch_shapes=[pltpu.VMEM((CLASS_ROWS, LANES), F32)],
        compiler_params=_params(("arbitrary",)),
        name="merge_route",
    )(x2, ya, yb, gta, gtb, wpa, wpb, wo, g1, n2, sc2, sh2, rwt, rb, tri)


def _lookup(table, idx):
    hit = idx[:, None] == jnp.arange(table.shape[0], dtype=jnp.int32)[None, :]
    return jnp.sum(jnp.where(hit, table[None, :], 0), axis=1)


def _moe_plan(counts, n):
    tm = TM_MOE
    tiles = n // tm + N_CLASSES
    padded = (counts + tm - 1) // tm * tm
    ends = jnp.cumsum(padded)
    bounds = jnp.concatenate([jnp.zeros((1,), jnp.int32), ends]).astype(jnp.int32)
    nvt = ends[-1] // tm
    pos = jnp.minimum(jnp.arange(tiles, dtype=jnp.int32), nvt - 1) * tm
    k = jnp.sum((ends[None, :] <= pos[:, None]).astype(jnp.int32), axis=1)
    ea = _lookup(jnp.asarray(_CLASS_EA, jnp.int32), k)
    eb = _lookup(jnp.asarray(_CLASS_EB, jnp.int32), k)
    first = jnp.ones((1,), jnp.int32)
    idx = jnp.arange(tiles, dtype=jnp.int32)

    def slot_plan(e):
        ch = jnp.concatenate([first, (e[1:] != e[:-1]).astype(jnp.int32)])
        at = jnp.where(ch == 1, idx, tiles)
        nxt = jnp.min(jnp.where(idx[None, :] > idx[:, None], at[None, :], tiles), axis=1)
        has = (nxt < tiles).astype(jnp.int32)
        return ch, _lookup(e, jnp.minimum(nxt, tiles - 1)), has

    cha, nxa, hasa = slot_plan(ea)
    chb, nxb, hasb = slot_plan(eb)
    return ea, eb, cha, chb, nxa, nxb, hasa, hasb, bounds, nvt.reshape(1).astype(jnp.int32)


def _moe_kernel(ea_ref, eb_ref, cha_ref, chb_ref, nxa_ref, nxb_ref, hasa_ref, hasb_ref, bnd_ref, cnt_ref, nvt_ref,
                pos_ref, hx_hbm, wg_hbm, wu_hbm, wd_hbm, y_ref,
                buf, wg_st, wu_st, wd_st, wg_bf, wu_bf, wd_bf, perm, gsem, wsem, *, n_tokens, layer):
    i = pl.program_id(0)
    tm = TM_MOE
    d = y_ref.shape[1]
    nvt = nvt_ref[0]

    def weight_copies(slot, e):
        return [pltpu.make_async_copy(src.at[layer, e], dst.at[slot], wsem.at[slot])
                for src, dst in ((wg_hbm, wg_st), (wu_hbm, wu_st), (wd_hbm, wd_st))]

    def refresh(slot, has_next, next_e):
        for cp in weight_copies(slot, 0):
            cp.wait()
        wg_bf[slot] = wg_st[slot].astype(BF16)
        wu_bf[slot] = wu_st[slot].astype(BF16)
        wd_bf[slot] = wd_st[slot].astype(BF16)

        @pl.when(has_next == 1)
        def _():
            for cp in weight_copies(slot, next_e):
                cp.start(priority=1)

    def gather_start(tile, slot):
        for r in range(tm):
            pltpu.make_async_copy(hx_hbm.at[pl.ds(perm[tile * tm + r], 1), :], buf.at[slot, pl.ds(r, 1), :],
                                  gsem.at[slot]).start()

    def gather_wait(slot):
        pltpu.make_async_copy(hx_hbm.at[pl.ds(0, tm), :], buf.at[slot], gsem.at[slot]).wait()

    @pl.when(i == 0)
    def _():
        for cp in weight_copies(0, ea_ref[0]) + weight_copies(1, eb_ref[0]):
            cp.start(priority=1)

        def place(t, carry):
            perm[pos_ref[t]] = t
            return carry

        lax.fori_loop(0, n_tokens, place, 0, unroll=8)

        def pad(p, carry):
            perm[p] = n_tokens - 1
            return carry

        for k in range(N_CLASSES):
            lax.fori_loop(bnd_ref[k] + cnt_ref[k], bnd_ref[k + 1], pad, 0)
        lax.fori_loop(bnd_ref[N_CLASSES], bnd_ref[N_CLASSES] + tm, pad, 0)
        gather_start(0, 0)

    @pl.when((i < nvt) & (cha_ref[i] == 1))
    def _():
        refresh(0, hasa_ref[i], nxa_ref[i])

    @pl.when((i < nvt) & (chb_ref[i] == 1))
    def _():
        refresh(1, hasb_ref[i], nxb_ref[i])

    @pl.when(i < nvt)
    def _():
        slot = lax.rem(i, 2)
        gather_wait(slot)
        gather_start(i + 1, 1 - slot)
        hx = buf[slot]
        h = hx[:, :d].astype(BF16)
        acc = None
        for s in range(2):
            w = hx[:, d + s:d + s + 1]
            gate = jnp.dot(h, wg_bf[s], preferred_element_type=F32)
            up = jnp.dot(h, wu_bf[s], preferred_element_type=F32)
            hid = (_silu(gate) * up * w).astype(BF16)
            part = jnp.dot(hid, wd_bf[s], preferred_element_type=F32)
            acc = part if acc is None else acc + part
        y_ref[...] = acc

    @pl.when(i >= nvt)
    def _():
        y_ref[...] = jnp.zeros_like(y_ref)

    @pl.when(i == nvt)
    def _():
        gather_wait(lax.rem(nvt, 2))


def _moe(hx, info, cnt, w_gate, w_up, w_down, layer):
    n, dx = hx.shape
    d = dx - LANES
    tm = TM_MOE
    tiles = n // tm + N_CLASSES
    counts = cnt[:, 0]
    ea, eb, cha, chb, nxa, nxb, hasa, hasb, bounds, nvt = _moe_plan(counts, n)
    pos = _lookup(bounds, info[:, 0, :].reshape(n)) + info[:, 1, :].reshape(n)
    f = w_gate.shape[-1]
    hbm = pl.BlockSpec(memory_space=pl.ANY)
    y = pl.pallas_call(
        functools.partial(_moe_kernel, n_tokens=n, layer=layer),
        out_shape=jax.ShapeDtypeStruct((tiles * tm, d), F32),
        grid_spec=pltpu.PrefetchScalarGridSpec(
            num_scalar_prefetch=12,
            grid=(tiles,),
            in_specs=[hbm, hbm, hbm, hbm],
            out_specs=pl.BlockSpec((tm, d), lambda i, *_: (i, 0)),
            scratch_shapes=[pltpu.VMEM((2, tm, dx), F32),
                            pltpu.VMEM((2, d, f), F32), pltpu.VMEM((2, d, f), F32), pltpu.VMEM((2, f, d), F32),
                            pltpu.VMEM((2, d, f), BF16), pltpu.VMEM((2, d, f), BF16), pltpu.VMEM((2, f, d), BF16),
                            pltpu.SMEM((tiles * tm,), jnp.int32),
                            pltpu.SemaphoreType.DMA((2,)), pltpu.SemaphoreType.DMA((2,))]),
        compiler_params=_params(("arbitrary",)),
        name="moe",
    )(ea, eb, cha, chb, nxa, nxb, hasa, hasb, bounds, counts, nvt, pos, hx, w_gate, w_up, w_down)
    return y, pos


def _final_kernel(pos_ref, x_ref, y_hbm, g2_ref, fin_ref, o_ref, ybuf, ysem):
    rows, finish = _gathered_rows(pos_ref, y_hbm, ybuf, ysem)
    x = x_ref[...] + g2_ref[0] * rows
    ms = jnp.mean(x * x, axis=-1, keepdims=True)
    o_ref[...] = x * lax.rsqrt(ms + EPS) * fin_ref[...]
    finish()


def _final(x2, y, pos, g2, fin, seq):
    n, d = x2.shape
    tm = TM_PROJ
    per_b = seq // tm
    row = lambda i, *_: (i, 0)
    return pl.pallas_call(
        _final_kernel,
        out_shape=jax.ShapeDtypeStruct((n, d), F32),
        grid_spec=pltpu.PrefetchScalarGridSpec(
            num_scalar_prefetch=1,
            grid=(n // tm,),
            in_specs=[pl.BlockSpec((tm, d), row), pl.BlockSpec(memory_space=pl.ANY),
                      pl.BlockSpec((1, 1, d), lambda i, *_: (i // per_b, 0, 0)),
                      pl.BlockSpec((1, d), lambda i, *_: (0, 0))],
            out_specs=pl.BlockSpec((tm, d), row),
            scratch_shapes=[pltpu.VMEM((2, tm, d), F32), pltpu.SemaphoreType.DMA((2,))]),
        compiler_params=_params(("arbitrary",)),
        name="final_norm",
    )(pos, x2, y, g2, fin)


def _pack_w_in(w_in_l):
    parts = jnp.split(w_in_l, [int(i) for i in np.cumsum(
        (GLA_DK, GLA_DK, GLA_DV, GLA_DV, GLA_RANK, SWA_DQ, SWA_DKV, SWA_DKV, D_MODEL))], axis=-1)
    q_a, k_a, v_a, r_a, lr_a, q_b, k_b, v_b, gt_a, gt_b = parts
    lr_pad = jnp.pad(lr_a, ((0, 0), (0, LANES - GLA_RANK)))
    return jnp.concatenate([q_a, k_a, v_a, r_a, q_b, k_b, v_b, gt_a, gt_b, lr_pad], axis=-1).astype(BF16)


def kernel(x, c, positions, ada_w, ada_b, norm1_g, norm2_g, final_g, w_in, gla_alpha_w, gla_alpha_b, gla_norm_g,
           swa_sinks, w_pa, w_pb, w_out, router_w, router_b, moe_w_gate, moe_w_up, moe_w_down):
    batch, seq, d = x.shape
    n = batch * seq
    cos, sin = _rope_tables(positions)
    mod = _modulation(c, ada_w, ada_b)
    rwt = router_w.T
    rb = router_b.reshape(N_EXPERTS, 1)
    x2 = x.reshape(n, d)
    y = pos = g2 = None
    for l in range(DEPTH):
        m = mod[l].reshape(batch, N_MOD, 1, d)
        sh1, sc1, g1, sh2, sc2, g2_l = (m[:, i] for i in range(N_MOD))
        outs = _inproj(x2, y, pos, g2, norm1_g[l][None, :], sc1, sh1, _pack_w_in(w_in[l]), cos, sin, seq)
        if y is not None:
            x2, outs = outs[0], outs[1:]
        qa, ka, va, ra, qb, kb, vb, gta, gtb, lr = outs
        aw = jnp.pad(gla_alpha_w[l], ((0, LANES - GLA_RANK), (0, 0))).astype(BF16)
        ya = _gla(qa, ka, va, ra, lr, aw, gla_alpha_b[l][None, :], gla_norm_g[l][None, :], batch, seq)
        yb = _swa(qb, kb, vb, swa_sinks[l], batch, seq)
        x2, hx, info, cnt = _merge(x2, ya, yb, gta, gtb, w_pa[l].astype(BF16), w_pb[l].astype(BF16),
                                   w_out[l].astype(BF16), g1, norm2_g[l][None, :], sc2, sh2, rwt, rb, seq)
        y, pos = _moe(hx, info, cnt, moe_w_gate, moe_w_up, moe_w_down, l)
        g2 = g2_l
    return _final(x2, y, pos, g2, final_g[None, :], seq).reshape(batch, seq, d)
```

```python
import functools

import jax
import jax.numpy as jnp
import numpy as np
from jax import lax
from jax.experimental import pallas as pl
from jax.experimental.pallas import tpu as pltpu

F32 = jnp.float32
BF16 = jnp.bfloat16

D_MODEL = 1024
DEPTH = 4
GLA_HEADS = 4
GLA_DK = D_MODEL // 2
GLA_DV = D_MODEL
GLA_HK = GLA_DK // GLA_HEADS
GLA_HV = GLA_DV // GLA_HEADS
GLA_RANK = 16
GLA_GATE_NORM = 16.0
GLA_CHUNK = 64
SWA_HEADS = 16
SWA_KV_HEADS = 2
HEAD_DIM = 64
SWA_GROUP = SWA_HEADS // SWA_KV_HEADS
SWA_DQ = SWA_HEADS * HEAD_DIM
SWA_DKV = SWA_KV_HEADS * HEAD_DIM
WINDOW = 128
ROPE_THETA = 10000.0
N_EXPERTS = 16
N_GROUPS = 4
EXPERTS_PER_GROUP = N_EXPERTS // N_GROUPS
D_EXPERT = 512
N_MOD = 6
EPS = 1e-6

LANES = 128
VMEM_LIMIT = 56 * 1024 * 1024
NEG_BIG = -1e30

_W_SEGS = (("qa", GLA_DK), ("ka", GLA_DK), ("va", GLA_DV), ("ra", GLA_DV), ("qb", SWA_DQ), ("kb", SWA_DKV),
           ("vb", SWA_DKV), ("gta", D_MODEL), ("gtb", D_MODEL), ("lr", LANES))
_W_OFF = {}
_o = 0
for _n, _w in _W_SEGS:
    _W_OFF[_n] = (_o, _o + _w)
    _o += _w
D_IN_PAD = _o

TM_PROJ = 512
TG_GLA = 256
TM_MOE = 256
TS_ROPE = 2048


def _params(sem):
    return pltpu.CompilerParams(dimension_semantics=sem, vmem_limit_bytes=VMEM_LIMIT)


def _split_bf16(a):
    hi = a.astype(BF16)
    lo = (a - hi.astype(F32)).astype(BF16)
    return hi, lo


def _sigmoid(x):
    return 1.0 / (1.0 + jnp.exp(-x))


def _silu(x):
    return x * _sigmoid(x)


def _rope_table_kernel(pos_ref, invf_ref, sign_ref, cos_ref, sin_ref):
    ang = pos_ref[...].astype(F32) * invf_ref[...]
    cos_ref[...] = jnp.cos(ang)
    sin_ref[...] = jnp.sin(ang) * sign_ref[...]


def _rope_tables(positions):
    n = positions.size
    half = HEAD_DIM // 2
    inv_freq = jnp.power(ROPE_THETA, -jnp.arange(half, dtype=F32) / half)
    invf = jnp.tile(inv_freq, LANES // half)[None, :]
    sign = jnp.tile(jnp.concatenate([-jnp.ones((half,), F32), jnp.ones((half,), F32)]), LANES // HEAD_DIM)[None, :]
    pos = positions.reshape(n, 1)
    return pl.pallas_call(
        _rope_table_kernel,
        out_shape=(jax.ShapeDtypeStruct((n, LANES), F32), jax.ShapeDtypeStruct((n, LANES), F32)),
        grid=(n // TS_ROPE,),
        in_specs=[pl.BlockSpec((TS_ROPE, 1), lambda i: (i, 0)),
                  pl.BlockSpec((1, LANES), lambda i: (0, 0)),
                  pl.BlockSpec((1, LANES), lambda i: (0, 0))],
        out_specs=(pl.BlockSpec((TS_ROPE, LANES), lambda i: (i, 0)),
                   pl.BlockSpec((TS_ROPE, LANES), lambda i: (i, 0))),
        compiler_params=_params(("arbitrary",)),
        name="rope_tables",
    )(pos, invf, sign)


def _mod_kernel(c_ref, w_ref, b_ref, o_ref):
    cond = _silu(c_ref[...])
    chi, clo = _split_bf16(cond)
    whi, wlo = _split_bf16(w_ref[0])
    acc = jnp.dot(chi, whi, preferred_element_type=F32)
    acc += jnp.dot(chi, wlo, preferred_element_type=F32)
    acc += jnp.dot(clo, whi, preferred_element_type=F32)
    o_ref[0] = acc + b_ref[0]


def _modulation(c, ada_w, ada_b):
    nl, d, n6 = ada_w.shape
    b = c.shape[0]
    tn = 1536
    return pl.pallas_call(
        _mod_kernel,
        out_shape=jax.ShapeDtypeStruct((nl, b, n6), F32),
        grid=(nl, n6 // tn),
        in_specs=[pl.BlockSpec((b, d), lambda l, j: (0, 0)),
                  pl.BlockSpec((1, d, tn), lambda l, j: (l, 0, j)),
                  pl.BlockSpec((1, 1, tn), lambda l, j: (l, 0, j))],
        out_specs=pl.BlockSpec((1, b, tn), lambda l, j: (l, 0, j)),
        compiler_params=_params(("arbitrary", "arbitrary")),
        name="adaln_mod",
    )(c, ada_w, ada_b.reshape(nl, 1, n6))


def _norm_mod(x, g, sc, sh):
    ms = jnp.mean(x * x, axis=-1, keepdims=True)
    return (x * lax.rsqrt(ms + EPS) * g) * (1.0 + sc) + sh


def _rope(t, cos, sin_signed):
    w = t.shape[-1]
    up = pltpu.roll(t, w - HEAD_DIM // 2, 1)
    dn = pltpu.roll(t, HEAD_DIM // 2, 1)
    lane = lax.broadcasted_iota(jnp.int32, t.shape, 1)
    swapped = jnp.where(lane % HEAD_DIM < HEAD_DIM // 2, up, dn)
    reps = w // LANES
    if reps > 1:
        cos = jnp.concatenate([cos] * reps, axis=1)
        sin_signed = jnp.concatenate([sin_signed] * reps, axis=1)
    return t * cos + swapped * sin_signed


def _gathered_rows(pos_ref, y_hbm, ybuf, ysem):
    i = pl.program_id(0)
    steps = pl.num_programs(0)
    tm = ybuf.shape[1]

    def start(step, slot):
        for r in range(tm):
            pltpu.make_async_copy(y_hbm.at[pl.ds(pos_ref[step * tm + r], 1), :], ybuf.at[slot, pl.ds(r, 1), :],
                                  ysem.at[slot]).start()

    def wait(slot):
        pltpu.make_async_copy(y_hbm.at[pl.ds(0, tm), :], ybuf.at[slot], ysem.at[slot]).wait()

    @pl.when(i == 0)
    def _():
        start(0, 0)

    slot = lax.rem(i, 2)
    wait(slot)
    start(lax.rem(i + 1, steps), 1 - slot)

    def finish():
        @pl.when(i == steps - 1)
        def _():
            wait(1 - slot)

    return ybuf[slot], finish


def _inproj_kernel(*refs, has_y):
    finish = None
    if has_y:
        pos_ref, x_ref, y_hbm, g2_ref = refs[:4]
        ybuf, ysem = refs[-2:]
        refs = refs[4:-2]
        rows, finish = _gathered_rows(pos_ref, y_hbm, ybuf, ysem)
        x = x_ref[...] + g2_ref[0] * rows
    else:
        x_ref = refs[0]
        refs = refs[1:]
        x = x_ref[...]
    g_ref, sc_ref, sh_ref, w_ref, cos_ref, sin_ref = refs[:6]
    outs = refs[6:]
    if has_y:
        outs[0][...] = x
        outs = outs[1:]
    qa_ref, ka_ref, va_ref, ra_ref, qb_ref, kb_ref, vb_ref, gta_ref, gtb_ref, lr_ref = outs
    h = _norm_mod(x, g_ref[...], sc_ref[0], sh_ref[0]).astype(BF16)

    def mm(name):
        c0, c1 = _W_OFF[name]
        return jnp.dot(h, w_ref[:, c0:c1], preferred_element_type=F32)

    qa_ref[...] = (mm("qa") * GLA_HK ** -0.5).astype(BF16)
    ka_ref[...] = mm("ka").astype(BF16)
    va_ref[...] = mm("va").astype(BF16)
    ra_ref[...] = mm("ra").astype(BF16)
    cos = cos_ref[...]
    sin = sin_ref[...]
    qb_ref[...] = (_rope(mm("qb"), cos, sin) * HEAD_DIM ** -0.5).astype(BF16)
    kb_ref[...] = _rope(mm("kb"), cos, sin).astype(BF16)
    vb_ref[...] = mm("vb").astype(BF16)
    gta_ref[...] = mm("gta").astype(BF16)
    gtb_ref[...] = mm("gtb").astype(BF16)
    lr_ref[...] = mm("lr")
    if finish is not None:
        finish()


def _inproj(x2, y, pos, g2, g, sc, sh, w, cos, sin, seq):
    n, d = x2.shape
    tm = TM_PROJ
    per_b = seq // tm
    row = lambda i, *_: (i, 0)
    const = lambda i, *_: (0, 0)
    bsel = lambda i, *_: (i // per_b, 0, 0)
    has_y = y is not None
    widths = [(wd, BF16) for _, wd in _W_SEGS[:-1]] + [(LANES, F32)]
    if has_y:
        widths = [(d, F32)] + widths
    resid_specs = [pl.BlockSpec(memory_space=pl.ANY), pl.BlockSpec((1, 1, d), bsel)] if has_y else []
    args = (pos, x2, y, g2) if has_y else (x2,)
    scratch = [pltpu.VMEM((2, tm, d), F32), pltpu.SemaphoreType.DMA((2,))] if has_y else []
    return pl.pallas_call(
        functools.partial(_inproj_kernel, has_y=has_y),
        out_shape=tuple(jax.ShapeDtypeStruct((n, wd), dt) for wd, dt in widths),
        grid_spec=pltpu.PrefetchScalarGridSpec(
            num_scalar_prefetch=1 if has_y else 0,
            grid=(n // tm,),
            in_specs=[pl.BlockSpec((tm, d), row)] + resid_specs + [
                      pl.BlockSpec((1, d), const),
                      pl.BlockSpec((1, 1, d), bsel),
                      pl.BlockSpec((1, 1, d), bsel),
                      pl.BlockSpec((d, D_IN_PAD), const, pipeline_mode=pl.Buffered(1)),
                      pl.BlockSpec((tm, LANES), row),
                      pl.BlockSpec((tm, LANES), row)],
            out_specs=tuple(pl.BlockSpec((tm, wd), row) for wd, _ in widths),
            scratch_shapes=scratch),
        compiler_params=_params(("arbitrary",)),
        name="norm_inproj",
    )(*args, g, sc, sh, w, cos, sin)


def _gla_kernel(q_ref, k_ref, v_ref, r_ref, lr_ref, aw_ref, ab_ref, gn_ref, tri_ref, o_ref, b_scr, st_scr):
    tg = q_ref.shape[1]
    c = GLA_CHUNK

    @pl.when(pl.program_id(1) == 0)
    def _():
        st_scr[...] = jnp.zeros_like(st_scr)

    pre = jnp.dot(lr_ref[0].astype(BF16), aw_ref[...], preferred_element_type=F32) + ab_ref[...]
    g = (jnp.minimum(pre, 0.0) - jnp.log1p(jnp.exp(-jnp.abs(pre)))) * (1.0 / GLA_GATE_NORM)
    ghi, glo = _split_bf16(g)
    tri = tri_ref[...]
    b_scr[...] = jnp.dot(tri, ghi, preferred_element_type=F32) + jnp.dot(tri, glo, preferred_element_type=F32)

    row = lax.broadcasted_iota(jnp.int32, (c, c), 0)
    col = lax.broadcasted_iota(jnp.int32, (c, c), 1)
    causal = col <= row
    gn = gn_ref[...]

    def chunk(ci, carry):
        r0 = pl.multiple_of(ci * c, c)
        rows = pl.ds(r0, c)
        for hh in range(GLA_HEADS):
            kc = slice(hh * GLA_HK, (hh + 1) * GLA_HK)
            vc = slice(hh * GLA_HV, (hh + 1) * GLA_HV)
            b = b_scr[rows, kc]
            b_last = b[c - 1:c, :]
            q = q_ref[0, rows, kc].astype(F32)
            k = k_ref[0, rows, kc].astype(F32)
            v = v_ref[0, rows, vc]
            q_e = (q * jnp.exp(b)).astype(BF16)
            k_i = (k * jnp.exp(-b)).astype(BF16)
            k_s = (k * jnp.exp(b_last - b)).astype(BF16)
            att = lax.dot_general(q_e, k_i, (((1,), (1,)), ((), ())), preferred_element_type=F32)
            att = jnp.where(causal, att, 0.0).astype(BF16)
            st = st_scr[hh]
            o = jnp.dot(att, v, preferred_element_type=F32)
            o += lax.dot_general(q_e, st.astype(BF16), (((1,), (1,)), ((), ())), preferred_element_type=F32)
            upd = lax.dot_general(v, k_s, (((0,), (0,)), ((), ())), preferred_element_type=F32)
            st_scr[hh] = st * jnp.exp(b_last) + upd
            ms = jnp.mean(o * o, axis=-1, keepdims=True)
            y = o * lax.rsqrt(ms + EPS) * gn
            r = r_ref[0, rows, vc].astype(F32)
            o_ref[0, rows, vc] = (y * _silu(r)).astype(BF16)
        return carry

    lax.fori_loop(0, tg // c, chunk, 0, unroll=True)


def _gla(qa, ka, va, ra, lr, aw, ab, gn, batch, seq):
    tg = TG_GLA
    blk = lambda b, s: (b, s, 0)
    const = lambda b, s: (0, 0)
    t = np.arange(tg)
    tri = jnp.asarray(((t[:, None] // GLA_CHUNK == t[None, :] // GLA_CHUNK) & (t[None, :] <= t[:, None])), BF16)
    r3 = lambda a: a.reshape(batch, seq, a.shape[-1])
    out = pl.pallas_call(
        _gla_kernel,
        out_shape=jax.ShapeDtypeStruct((batch, seq, GLA_DV), BF16),
        grid=(batch, seq // tg),
        in_specs=[pl.BlockSpec((1, tg, GLA_DK), blk),
                  pl.BlockSpec((1, tg, GLA_DK), blk),
                  pl.BlockSpec((1, tg, GLA_DV), blk),
                  pl.BlockSpec((1, tg, GLA_DV), blk),
                  pl.BlockSpec((1, tg, LANES), blk),
                  pl.BlockSpec((LANES, GLA_DK), const),
                  pl.BlockSpec((1, GLA_DK), const),
                  pl.BlockSpec((1, GLA_HV), const),
                  pl.BlockSpec((tg, tg), const)],
        out_specs=pl.BlockSpec((1, tg, GLA_DV), blk),
        scratch_shapes=[pltpu.VMEM((tg, GLA_DK), F32), pltpu.VMEM((GLA_HEADS, GLA_HV, GLA_HK), F32)],
        compiler_params=_params(("arbitrary", "arbitrary")),
        name="gla",
    )(r3(qa), r3(ka), r3(va), r3(ra), r3(lr), aw, ab, gn, tri)
    return out.reshape(batch * seq, GLA_DV)


def _swa_kernel(sink_ref, q_ref, kp_ref, kc_ref, vp_ref, vc_ref, o_ref, s_scr, p_scr):
    w = WINDOW
    blk = pl.program_id(1)
    t = lax.broadcasted_iota(jnp.int32, (w, 2 * w), 0)
    j = lax.broadcasted_iota(jnp.int32, (w, 2 * w), 1)
    diff = w + t - j
    valid = (diff >= 0) & (diff < w) & (blk * w + j - w >= 0)
    bias = jnp.where(valid, 0.0, NEG_BIG)

    lane = lax.broadcasted_iota(jnp.int32, (2 * w, LANES), 1)
    low = lane < HEAD_DIM

    def halves(prev_ref, cur_ref):
        both = jnp.concatenate([prev_ref[0], cur_ref[0]], axis=0)
        swapped = pltpu.roll(both, HEAD_DIM, 1)
        zero = jnp.zeros_like(both)
        kv0 = (jnp.where(low, both, zero), jnp.where(low, zero, swapped))
        kv1 = (jnp.where(low, swapped, zero), jnp.where(low, zero, both))
        return kv0, kv1

    k_halves = halves(kp_ref, kc_ref)
    v_halves = halves(vp_ref, vc_ref)
    pairs = SWA_GROUP // 2
    nt = (((1,), (1,)), ((), ()))
    for c in range(SWA_KV_HEADS):
        q4 = jnp.concatenate([q_ref[0, :, (c * pairs + p) * LANES:(c * pairs + p + 1) * LANES]
                              for p in range(pairs)], axis=0)
        for par in range(2):
            s_scr[par] = lax.dot_general(q4, k_halves[c][par], nt, preferred_element_type=F32)
        for par in range(2):
            for p in range(pairs):
                sink = sink_ref[c * SWA_GROUP + 2 * p + par]
                rows = slice(p * w, (p + 1) * w)
                s = s_scr[par, rows, :] + bias
                m = jnp.maximum(jnp.max(s, axis=-1, keepdims=True), sink)
                e = jnp.exp(s - m)
                denom = jnp.sum(e, axis=-1, keepdims=True) + jnp.exp(sink - m)
                p_scr[par, rows, :] = (e * (1.0 / denom)).astype(BF16)
        o4 = (jnp.dot(p_scr[0], v_halves[c][0], preferred_element_type=F32)
              + jnp.dot(p_scr[1], v_halves[c][1], preferred_element_type=F32))
        for p in range(pairs):
            o_ref[0, :, (c * pairs + p) * LANES:(c * pairs + p + 1) * LANES] = o4[p * w:(p + 1) * w].astype(BF16)


def _swa(qb, kb, vb, sinks, batch, seq):
    w = WINDOW
    r3 = lambda a: a.reshape(batch, seq, a.shape[-1])
    cur = lambda b, s, sk: (b, s, 0)
    prev = lambda b, s, sk: (b, jnp.maximum(s - 1, 0), 0)
    out = pl.pallas_call(
        _swa_kernel,
        out_shape=jax.ShapeDtypeStruct((batch, seq, SWA_DQ), BF16),
        grid_spec=pltpu.PrefetchScalarGridSpec(
            num_scalar_prefetch=1,
            grid=(batch, seq // w),
            in_specs=[pl.BlockSpec((1, w, SWA_DQ), cur),
                      pl.BlockSpec((1, w, SWA_DKV), prev),
                      pl.BlockSpec((1, w, SWA_DKV), cur),
                      pl.BlockSpec((1, w, SWA_DKV), prev),
                      pl.BlockSpec((1, w, SWA_DKV), cur)],
            out_specs=pl.BlockSpec((1, w, SWA_DQ), cur),
            scratch_shapes=[pltpu.VMEM((2, SWA_GROUP // 2 * w, 2 * w), F32),
                            pltpu.VMEM((2, SWA_GROUP // 2 * w, 2 * w), BF16)]),
        compiler_params=_params(("arbitrary", "arbitrary")),
        name="swa",
    )(sinks, r3(qb), r3(kb), r3(kb), r3(vb), r3(vb))
    return out.reshape(batch * seq, SWA_DQ)


def _top2_sum(a):
    best = a[0] + a[1]
    for i in range(len(a)):
        for j in range(i + 1, len(a)):
            if (i, j) != (0, 1):
                best = jnp.maximum(best, a[i] + a[j])
    return best


_PAIR_SLOTS = ((0, 1), (2, 1), (2, 0), (3, 0), (3, 1), (3, 2))
N_CLASSES = N_GROUPS * len(_PAIR_SLOTS)
CLASS_ROWS = 32
_CLASS_EA = tuple(g * EXPERTS_PER_GROUP + a for g in range(N_GROUPS) for a, _ in _PAIR_SLOTS)
_CLASS_EB = tuple(g * EXPERTS_PER_GROUP + b for g in range(N_GROUPS) for _, b in _PAIR_SLOTS)


def _route(logits_t, rb):
    scores = _sigmoid(logits_t)
    biased = scores + rb
    sc = [scores[e:e + 1, :] for e in range(N_EXPERTS)]
    bi = [biased[e:e + 1, :] for e in range(N_EXPERTS)]
    gs = [_top2_sum(bi[g * EXPERTS_PER_GROUP:(g + 1) * EXPERTS_PER_GROUP]) for g in range(N_GROUPS)]
    best = gs[0]
    for g in range(1, N_GROUPS):
        best = jnp.maximum(best, gs[g])
    taken = jnp.zeros_like(best, dtype=jnp.bool_)
    in_grp = []
    for g in range(N_GROUPS):
        sel = (gs[g] == best) & jnp.logical_not(taken)
        taken = taken | sel
        in_grp.append(sel)
    cand = [jnp.where(in_grp[e // EXPERTS_PER_GROUP], bi[e], -jnp.inf) for e in range(N_EXPERTS)]
    picked = [None] * N_EXPERTS
    chosen = [jnp.zeros_like(taken) for _ in range(N_EXPERTS)]
    for _ in range(2):
        top = cand[0]
        for e in range(1, N_EXPERTS):
            top = jnp.maximum(top, cand[e])
        done = jnp.zeros_like(taken)
        for e in range(N_EXPERTS):
            hit = (cand[e] == top) & jnp.logical_not(done)
            done = done | hit
            chosen[e] = chosen[e] | hit
            cand[e] = jnp.where(hit, -jnp.inf, cand[e])
    total = jnp.zeros_like(best)
    for e in range(N_EXPERTS):
        picked[e] = jnp.where(chosen[e], sc[e], 0.0)
        total = total + picked[e]
    ind = [chosen[_CLASS_EA[k]] & chosen[_CLASS_EB[k]] for k in range(N_CLASSES)]
    w_a = jnp.zeros_like(best)
    w_b = jnp.zeros_like(best)
    for k in range(N_CLASSES):
        w_a = w_a + jnp.where(ind[k], picked[_CLASS_EA[k]], 0.0)
        w_b = w_b + jnp.where(ind[k], picked[_CLASS_EB[k]], 0.0)
    ind_rows = [jnp.where(i, 1.0, 0.0) for i in ind] + [jnp.zeros_like(best)] * (CLASS_ROWS - N_CLASSES)
    return jnp.concatenate(ind_rows, axis=0), w_a / total, w_b / total


def _merge_kernel(x_ref, ya_ref, yb_ref, gta_ref, gtb_ref, wpa_ref, wpb_ref, wo_ref, g1_ref, n2_ref, sc2_ref, sh2_ref,
                  rwt_ref, rb_ref, tri_ref, xo_ref, hx_ref, info_ref, cnt_ref, carry_scr):
    @pl.when(pl.program_id(0) == 0)
    def _():
        carry_scr[...] = jnp.zeros_like(carry_scr)

    pa = jnp.dot(ya_ref[...], wpa_ref[...], preferred_element_type=F32)
    pb = jnp.dot(yb_ref[...], wpb_ref[...], preferred_element_type=F32)
    merged = _sigmoid(gta_ref[...].astype(F32)) * pa + _sigmoid(gtb_ref[...].astype(F32)) * pb
    mix = jnp.dot(merged.astype(BF16), wo_ref[...], preferred_element_type=F32)
    x = x_ref[...] + g1_ref[0] * mix
    xo_ref[...] = x
    h2 = _norm_mod(x, n2_ref[...], sc2_ref[0], sh2_ref[0])
    d = h2.shape[1]
    tm = h2.shape[0]
    hhi, hlo = _split_bf16(h2)
    whi, wlo = _split_bf16(rwt_ref[...])
    nt = (((1,), (1,)), ((), ()))
    logits_t = (lax.dot_general(whi, hhi, nt, preferred_element_type=F32)
                + lax.dot_general(whi, hlo, nt, preferred_element_type=F32)
                + lax.dot_general(wlo, hhi, nt, preferred_element_type=F32))
    ind, w_a, w_b = _route(logits_t, rb_ref[...])
    hx_ref[:, :d] = h2
    hx_ref[:, d:] = jnp.concatenate([w_a, w_b, jnp.zeros((LANES - 2, tm), F32)], axis=0).T
    carry = carry_scr[:, 0:1]
    cum = jnp.dot(ind.astype(BF16), tri_ref[...], preferred_element_type=F32)
    k_col = lax.broadcasted_iota(jnp.int32, (CLASS_ROWS, 1), 0).astype(F32)
    cls = jnp.sum(ind * k_col, axis=0, keepdims=True)
    rank = jnp.sum(ind * (cum + carry), axis=0, keepdims=True) - 1.0
    info = jnp.concatenate([cls, rank, jnp.zeros((6, tm), F32)], axis=0)
    info_ref[0] = info.astype(jnp.int32)
    carry = carry + jnp.sum(ind, axis=1, keepdims=True)
    carry_scr[...] = jnp.broadcast_to(carry, carry_scr.shape)
    cnt_ref[...] = jnp.broadcast_to(carry, cnt_ref.shape).astype(jnp.int32)


def _merge(x2, ya, yb, gta, gtb, wpa, wpb, wo, g1, n2, sc2, sh2, rwt, rb, seq):
    n, d = x2.shape
    tm = TM_PROJ
    per_b = seq // tm
    row = lambda i: (i, 0)
    const = lambda i: (0, 0)
    bsel = lambda i: (i // per_b, 0, 0)
    wspec = pl.BlockSpec((d, d), const, pipeline_mode=pl.Buffered(1))
    t = np.arange(tm)
    tri = jnp.asarray(t[:, None] <= t[None, :], BF16)
    return pl.pallas_call(
        _merge_kernel,
        out_shape=(jax.ShapeDtypeStruct((n, d), F32), jax.ShapeDtypeStruct((n, d + LANES), F32),
                   jax.ShapeDtypeStruct((n // tm, 8, tm), jnp.int32),
                   jax.ShapeDtypeStruct((CLASS_ROWS, LANES), jnp.int32)),
        grid=(n // tm,),
        in_specs=[pl.BlockSpec((tm, d), row)] * 5 + [wspec, wspec, wspec,
                  pl.BlockSpec((1, 1, d), bsel),
                  pl.BlockSpec((1, d), const),
                  pl.BlockSpec((1, 1, d), bsel),
                  pl.BlockSpec((1, 1, d), bsel),
                  pl.BlockSpec((N_EXPERTS, d), const),
                  pl.BlockSpec((N_EXPERTS, 1), const),
                  pl.BlockSpec((tm, tm), const)],
        out_specs=(pl.BlockSpec((tm, d), row), pl.BlockSpec((tm, d + LANES), row),
                   pl.BlockSpec((1, 8, tm), lambda i: (i, 0, 0)),
                   pl.BlockSpec((CLASS_ROWS, LANES), const)),
        scratch_shapes=[pltpu.VMEM((CLASS_ROWS, LANES), F32)],
        compiler_params=_params(("arbitrary",)),
        name="merge_route",
    )(x2, ya, yb, gta, gtb, wpa, wpb, wo, g1, n2, sc2, sh2, rwt, rb, tri)


def _lookup(table, idx):
    hit = idx[:, None] == jnp.arange(table.shape[0], dtype=jnp.int32)[None, :]
    return jnp.sum(jnp.where(hit, table[None, :], 0), axis=1)


def _moe_plan(counts, n):
    tm = TM_MOE
    tiles = n // tm + N_CLASSES
    padded = (counts + tm - 1) // tm * tm
    ends = jnp.cumsum(padded)
    bounds = jnp.concatenate([jnp.zeros((1,), jnp.int32), ends]).astype(jnp.int32)
    nvt = ends[-1] // tm
    pos = jnp.minimum(jnp.arange(tiles, dtype=jnp.int32), nvt - 1) * tm
    k = jnp.sum((ends[None, :] <= pos[:, None]).astype(jnp.int32), axis=1)
    ea = _lookup(jnp.asarray(_CLASS_EA, jnp.int32), k)
    eb = _lookup(jnp.asarray(_CLASS_EB, jnp.int32), k)
    first = jnp.ones((1,), jnp.int32)
    idx = jnp.arange(tiles, dtype=jnp.int32)

    def slot_plan(e):
        ch = jnp.concatenate([first, (e[1:] != e[:-1]).astype(jnp.int32)])
        at = jnp.where(ch == 1, idx, tiles)
        nxt = jnp.min(jnp.where(idx[None, :] > idx[:, None], at[None, :], tiles), axis=1)
        has = (nxt < tiles).astype(jnp.int32)
        return ch, _lookup(e, jnp.minimum(nxt, tiles - 1)), has

    cha, nxa, hasa = slot_plan(ea)
    chb, nxb, hasb = slot_plan(eb)
    return ea, eb, cha, chb, nxa, nxb, hasa, hasb, bounds, nvt.reshape(1).astype(jnp.int32)


def _moe_kernel(ea_ref, eb_ref, cha_ref, chb_ref, nxa_ref, nxb_ref, hasa_ref, hasb_ref, bnd_ref, cnt_ref, nvt_ref,
                pos_ref, hx_hbm, wg_hbm, wu_hbm, wd_hbm, y_ref,
                buf, wg_st, wu_st, wd_st, wg_bf, wu_bf, wd_bf, perm, gsem, wsem, *, n_tokens, layer):
    i = pl.program_id(0)
    tm = TM_MOE
    d = y_ref.shape[1]
    nvt = nvt_ref[0]

    def weight_copies(slot, e):
        return [pltpu.make_async_copy(src.at[layer, e], dst.at[slot], wsem.at[slot])
                for src, dst in ((wg_hbm, wg_st), (wu_hbm, wu_st), (wd_hbm, wd_st))]

    def refresh(slot, has_next, next_e):
        for cp in weight_copies(slot, 0):
            cp.wait()
        wg_bf[slot] = wg_st[slot].astype(BF16)
        wu_bf[slot] = wu_st[slot].astype(BF16)
        wd_bf[slot] = wd_st[slot].astype(BF16)

        @pl.when(has_next == 1)
        def _():
            for cp in weight_copies(slot, next_e):
                cp.start(priority=1)

    def gather_start(tile, slot):
        for r in range(tm):
            pltpu.make_async_copy(hx_hbm.at[pl.ds(perm[tile * tm + r], 1), :], buf.at[slot, pl.ds(r, 1), :],
                                  gsem.at[slot]).start(priority=r % 2)

    def gather_wait(slot):
        pltpu.make_async_copy(hx_hbm.at[pl.ds(0, tm), :], buf.at[slot], gsem.at[slot]).wait()

    @pl.when(i == 0)
    def _():
        for cp in weight_copies(0, ea_ref[0]) + weight_copies(1, eb_ref[0]):
            cp.start(priority=1)

        def place(t, carry):
            perm[pos_ref[t]] = t
            return carry

        lax.fori_loop(0, n_tokens, place, 0, unroll=8)

        def pad(p, carry):
            perm[p] = n_tokens - 1
            return carry

        for k in range(N_CLASSES):
            lax.fori_loop(bnd_ref[k] + cnt_ref[k], bnd_ref[k + 1], pad, 0)
        lax.fori_loop(bnd_ref[N_CLASSES], bnd_ref[N_CLASSES] + tm, pad, 0)
        gather_start(0, 0)

    @pl.when((i < nvt) & (cha_ref[i] == 1))
    def _():
        refresh(0, hasa_ref[i], nxa_ref[i])

    @pl.when((i < nvt) & (chb_ref[i] == 1))
    def _():
        refresh(1, hasb_ref[i], nxb_ref[i])

    @pl.when(i < nvt)
    def _():
        slot = lax.rem(i, 2)
        gather_wait(slot)
        gather_start(i + 1, 1 - slot)
        hx = buf[slot]
        h = hx[:, :d].astype(BF16)
        acc = None
        for s in range(2):
            w = hx[:, d + s:d + s + 1]
            gate = jnp.dot(h, wg_bf[s], preferred_element_type=F32)
            up = jnp.dot(h, wu_bf[s], preferred_element_type=F32)
            hid = (_silu(gate) * up * w).astype(BF16)
            part = jnp.dot(hid, wd_bf[s], preferred_element_type=F32)
            acc = part if acc is None else acc + part
        y_ref[...] = acc

    @pl.when(i >= nvt)
    def _():
        y_ref[...] = jnp.zeros_like(y_ref)

    @pl.when(i == nvt)
    def _():
        gather_wait(lax.rem(nvt, 2))


def _moe(hx, info, cnt, w_gate, w_up, w_down, layer):
    n, dx = hx.shape
    d = dx - LANES
    tm = TM_MOE
    tiles = n // tm + N_CLASSES
    counts = cnt[:, 0]
    ea, eb, cha, chb, nxa, nxb, hasa, hasb, bounds, nvt = _moe_plan(counts, n)
    pos = _lookup(bounds, info[:, 0, :].reshape(n)) + info[:, 1, :].reshape(n)
    f = w_gate.shape[-1]
    hbm = pl.BlockSpec(memory_space=pl.ANY)
    y = pl.pallas_call(
        functools.partial(_moe_kernel, n_tokens=n, layer=layer),
        out_shape=jax.ShapeDtypeStruct((tiles * tm, d), F32),
        grid_spec=pltpu.PrefetchScalarGridSpec(
            num_scalar_prefetch=12,
            grid=(tiles,),
            in_specs=[hbm, hbm, hbm, hbm],
            out_specs=pl.BlockSpec((tm, d), lambda i, *_: (i, 0)),
            scratch_shapes=[pltpu.VMEM((2, tm, dx), F32),
                            pltpu.VMEM((2, d, f), F32), pltpu.VMEM((2, d, f), F32), pltpu.VMEM((2, f, d), F32),
                            pltpu.VMEM((2, d, f), BF16), pltpu.VMEM((2, d, f), BF16), pltpu.VMEM((2, f, d), BF16),
                            pltpu.SMEM((tiles * tm,), jnp.int32),
                            pltpu.SemaphoreType.DMA((2,)), pltpu.SemaphoreType.DMA((2,))]),
        compiler_params=_params(("arbitrary",)),
        name="moe",
    )(ea, eb, cha, chb, nxa, nxb, hasa, hasb, bounds, counts, nvt, pos, hx, w_gate, w_up, w_down)
    return y, pos


def _final_kernel(pos_ref, x_ref, y_hbm, g2_ref, fin_ref, o_ref, ybuf, ysem):
    rows, finish = _gathered_rows(pos_ref, y_hbm, ybuf, ysem)
    x = x_ref[...] + g2_ref[0] * rows
    ms = jnp.mean(x * x, axis=-1, keepdims=True)
    o_ref[...] = x * lax.rsqrt(ms + EPS) * fin_ref[...]
    finish()


def _final(x2, y, pos, g2, fin, seq):
    n, d = x2.shape
    tm = TM_PROJ
    per_b = seq // tm
    row = lambda i, *_: (i, 0)
    return pl.pallas_call(
        _final_kernel,
        out_shape=jax.ShapeDtypeStruct((n, d), F32),
        grid_spec=pltpu.PrefetchScalarGridSpec(
            num_scalar_prefetch=1,
            grid=(n // tm,),
            in_specs=[pl.BlockSpec((tm, d), row), pl.BlockSpec(memory_space=pl.ANY),
                      pl.BlockSpec((1, 1, d), lambda i, *_: (i // per_b, 0, 0)),
                      pl.BlockSpec((1, d), lambda i, *_: (0, 0))],
            out_specs=pl.BlockSpec((tm, d), row),
            scratch_shapes=[pltpu.VMEM((2, tm, d), F32), pltpu.SemaphoreType.DMA((2,))]),
        compiler_params=_params(("arbitrary",)),
        name="final_norm",
    )(pos, x2, y, g2, fin)


def _pack_w_in(w_in_l):
    parts = jnp.split(w_in_l, [int(i) for i in np.cumsum(
        (GLA_DK, GLA_DK, GLA_DV, GLA_DV, GLA_RANK, SWA_DQ, SWA_DKV, SWA_DKV, D_MODEL))], axis=-1)
    q_a, k_a, v_a, r_a, lr_a, q_b, k_b, v_b, gt_a, gt_b = parts
    lr_pad = jnp.pad(lr_a, ((0, 0), (0, LANES - GLA_RANK)))
    return jnp.concatenate([q_a, k_a, v_a, r_a, q_b, k_b, v_b, gt_a, gt_b, lr_pad], axis=-1).astype(BF16)


def kernel(x, c, positions, ada_w, ada_b, norm1_g, norm2_g, final_g, w_in, gla_alpha_w, gla_alpha_b, gla_norm_g,
           swa_sinks, w_pa, w_pb, w_out, router_w, router_b, moe_w_gate, moe_w_up, moe_w_down):
    batch, seq, d = x.shape
    n = batch * seq
    cos, sin = _rope_tables(positions)
    mod = _modulation(c, ada_w, ada_b)
    rwt = router_w.T
    rb = router_b.reshape(N_EXPERTS, 1)
    x2 = x.reshape(n, d)
    y = pos = g2 = None
    for l in range(DEPTH):
        m = mod[l].reshape(batch, N_MOD, 1, d)
        sh1, sc1, g1, sh2, sc2, g2_l = (m[:, i] for i in range(N_MOD))
        outs = _inproj(x2, y, pos, g2, norm1_g[l][None, :], sc1, sh1, _pack_w_in(w_in[l]), cos, sin, seq)
        if y is not None:
            x2, outs = outs[0], outs[1:]
        qa, ka, va, ra, qb, kb, vb, gta, gtb, lr = outs
        aw = jnp.pad(gla_alpha_w[l], ((0, LANES - GLA_RANK), (0, 0))).astype(BF16)
        ya = _gla(qa, ka, va, ra, lr, aw, gla_alpha_b[l][None, :], gla_norm_g[l][None, :], batch, seq)
        yb = _swa(qb, kb, vb, swa_sinks[l], batch, seq)
        x2, hx, info, cnt = _merge(x2, ya, yb, gta, gtb, w_pa[l].astype(BF16), w_pb[l].astype(BF16),
                                   w_out[l].astype(BF16), g1, norm2_g[l][None, :], sc2, sh2, rwt, rb, seq)
        y, pos = _moe(hx, info, cnt, moe_w_gate, moe_w_up, moe_w_down, l)
        g2 = g2_l
    return _final(x2, y, pos, g2, final_g[None, :], seq).reshape(batch, seq, d)
```
